```python
import jax, jax.numpy as jnp
from jax import lax
import numpy as np

D_MODEL = 1024
BATCH = 8
SEQ = 4096
DEPTH = 1
DEC_BATCH = 8
DEC_SEQ = 64
PAST_LEN = 1024

CHUNK = 64
Q_BLOCK = 128
CONV_WIDTH = 3
D_CONV = D_MODEL
N_HEADS = 16
QK_NOPE = 64
QK_ROPE = 32
V_HEAD = 64
Q_LORA = 512
KV_LORA = 256
ROPE_THETA = 10000.0
N_MEM = 256
MEM_HEADS = 4
MEM_HEAD_DIM = D_MODEL // MEM_HEADS
D_FF = 4 * D_MODEL
EPS = 1e-6
IN_COLS = 3 * D_CONV + Q_LORA + KV_LORA + QK_ROPE + 2 * D_MODEL
ATTN_SCALE = (QK_NOPE + QK_ROPE) ** -0.5
MEM_SCALE = MEM_HEAD_DIM ** -0.5

kernel_name = 'hybrid_shortconv_mla_stream_step'


def rms_norm(x, g):
    x32 = x.astype(jnp.float32)
    y = x32 * lax.rsqrt(jnp.mean(x32 * x32, axis=-1, keepdims=True) + EPS)
    return (y * g.astype(jnp.float32)).astype(x.dtype)


def rope(x, pos):
    half = QK_ROPE // 2
    inv = ROPE_THETA ** (-jnp.arange(half, dtype=jnp.float32) / half)
    ang = pos.astype(jnp.float32)[:, None] * inv[None, :]
    ang = ang.reshape((ang.shape[0],) + (1,) * (x.ndim - 3) + (half,))
    cos, sin = jnp.cos(ang), jnp.sin(ang)
    x32 = x.astype(jnp.float32)
    x1, x2 = x32[..., :half], x32[..., half:]
    return jnp.concatenate([x1 * cos - x2 * sin, x2 * cos + x1 * sin], axis=-1).astype(x.dtype)


def split_in(z):
    sizes = (D_CONV, D_CONV, D_CONV, Q_LORA, KV_LORA, QK_ROPE, D_MODEL, D_MODEL)
    idx = np.cumsum(sizes)[:-1].tolist()
    return jnp.split(z, idx, axis=-1)


def short_conv(v, buf, w_conv):
    T = v.shape[1]
    vp = jnp.concatenate([buf.astype(v.dtype), v], axis=1)
    y = vp[:, 0:T] * w_conv[0]
    for k in range(1, CONV_WIDTH):
        y = y + vp[:, k:k + T] * w_conv[k]
    return y, vp[:, T:]


def mla_keys_values(ckv, kr, w_ukv):
    Bn, L = ckv.shape[:2]
    kv = (ckv @ w_ukv).reshape(Bn, L, N_HEADS, QK_NOPE + V_HEAD)
    k = jnp.concatenate([kv[..., :QK_NOPE], jnp.broadcast_to(kr[:, :, None, :], (Bn, L, N_HEADS, QK_ROPE))], axis=-1)
    return k, kv[..., QK_NOPE:]


def chunk_causal_attend(q, k, v, q_pos, k_pos):
    s = jnp.einsum('bqhd,bkhd->bhqk', q, k).astype(jnp.float32) * ATTN_SCALE
    visible = (k_pos[None, :] // CHUNK) <= (q_pos[:, None] // CHUNK)
    p = jax.nn.softmax(jnp.where(visible, s, -jnp.inf), axis=-1).astype(v.dtype)
    return jnp.einsum('bhqk,bkhd->bqhd', p, v)


def blocked_attend(q, k, v, k_pos):
    Bn, T = q.shape[:2]
    nblk = T // Q_BLOCK
    qb = q.reshape(Bn, nblk, Q_BLOCK, N_HEADS, q.shape[-1]).transpose(1, 0, 2, 3, 4)
    starts = jnp.arange(nblk, dtype=jnp.int32) * Q_BLOCK

    def one(args):
        q_blk, start = args
        return chunk_causal_attend(q_blk, k, v, start + jnp.arange(Q_BLOCK, dtype=jnp.int32), k_pos)

    o = lax.map(one, (qb, starts))
    return o.transpose(1, 0, 2, 3, 4).reshape(Bn, T, N_HEADS, V_HEAD)


def memory_kv(mem, g, w_k, w_v):
    Bn, M = mem.shape[:2]
    mn = rms_norm(mem, g)
    k = (mn @ w_k).reshape(Bn, M, MEM_HEADS, MEM_HEAD_DIM)
    v = (mn @ w_v).reshape(Bn, M, MEM_HEADS, MEM_HEAD_DIM)
    return k, v


def memory_attend(h, mem_k, mem_v, w_q, w_o):
    Bn, T = h.shape[:2]
    q = (h @ w_q).reshape(Bn, T, MEM_HEADS, MEM_HEAD_DIM)
    s = jnp.einsum('bqhd,bkhd->bhqk', q, mem_k).astype(jnp.float32) * MEM_SCALE
    p = jax.nn.softmax(s, axis=-1).astype(mem_v.dtype)
    o = jnp.einsum('bhqk,bkhd->bqhd', p, mem_v).reshape(Bn, T, MEM_HEADS * MEM_HEAD_DIM)
    return o @ w_o


def layer(x, conv_buf, past_ckv, past_kr, mem_k, mem_v, blocked,
          g_mix, w_in, w_conv, w_conv_out, g_q, w_uq, g_kv, w_ukv, w_mla_out, w_mix_out,
          g_mem_q, w_qm, w_om, g_mlp, w_up, w_down):
    Bn, T = x.shape[:2]
    past = past_ckv.shape[1]
    pos = past + jnp.arange(T, dtype=jnp.int32)
    k_pos = jnp.arange(past + T, dtype=jnp.int32)
    n = rms_norm(x, g_mix)
    u, gate_b, gate_c, cq, ckv_raw, kr_raw, a_conv, a_mla = split_in(n @ w_in)
    y_sc, new_buf = short_conv(gate_c * u, conv_buf, w_conv)
    y_a = (gate_b * y_sc) @ w_conv_out
    q = (rms_norm(cq, g_q) @ w_uq).reshape(Bn, T, N_HEADS, QK_NOPE + QK_ROPE)
    q = jnp.concatenate([q[..., :QK_NOPE], rope(q[..., QK_NOPE:], pos)], axis=-1)
    ckv = rms_norm(ckv_raw, g_kv)
    kr = rope(kr_raw, pos)
    k, v = mla_keys_values(jnp.concatenate([past_ckv.astype(ckv.dtype), ckv], axis=1),
                           jnp.concatenate([past_kr.astype(kr.dtype), kr], axis=1), w_ukv)
    if blocked:
        o = blocked_attend(q, k, v, k_pos)
    else:
        o = chunk_causal_attend(q, k, v, pos, k_pos)
    y_b = o.reshape(Bn, T, N_HEADS * V_HEAD) @ w_mla_out
    x = x + (jax.nn.sigmoid(a_conv) * y_a + jax.nn.sigmoid(a_mla) * y_b) @ w_mix_out
    x = x + memory_attend(rms_norm(x, g_mem_q), mem_k, mem_v, w_qm, w_om)
    hm = rms_norm(x, g_mlp)
    x = x + jnp.square(jax.nn.relu(hm @ w_up)) @ w_down
    return x, new_buf, ckv, kr


def setup_inputs(seed: int = 0) -> dict:
    key = jax.random.key(seed)
    ks = jax.random.split(key, 32)

    def nrm(k, shape, scale):
        return jax.random.normal(k, shape, jnp.float32) * scale

    def gain(k, shape):
        return 1.0 + 0.01 * jax.random.normal(k, shape, jnp.float32)

    return {
        'x_prompt': nrm(ks[0], (BATCH, SEQ, D_MODEL), 1.0),
        'x_sample': nrm(ks[1], (DEC_BATCH, DEC_SEQ, D_MODEL), 1.0),
        'cache_conv': nrm(ks[2], (DEPTH, DEC_BATCH, CONV_WIDTH - 1, D_CONV), 1.0),
        'cache_ckv': nrm(ks[3], (DEPTH, DEC_BATCH, PAST_LEN, KV_LORA), 1.0),
        'cache_krope': nrm(ks[4], (DEPTH, DEC_BATCH, PAST_LEN, QK_ROPE), 1.0),
        'cache_mem_k': nrm(ks[5], (DEPTH, DEC_BATCH, N_MEM, MEM_HEADS, MEM_HEAD_DIM), 1.0),
        'cache_mem_v': nrm(ks[6], (DEPTH, DEC_BATCH, N_MEM, MEM_HEADS, MEM_HEAD_DIM), 1.0),
        'mem_prompt': nrm(ks[7], (BATCH, N_MEM, D_MODEL), 1.0),
        'g_mix': gain(ks[8], (DEPTH, D_MODEL)),
        'w_in': nrm(ks[9], (DEPTH, D_MODEL, IN_COLS), D_MODEL ** -0.5),
        'w_conv': nrm(ks[10], (DEPTH, CONV_WIDTH, D_CONV), CONV_WIDTH ** -0.5),
        'w_conv_out': nrm(ks[11], (DEPTH, D_CONV, D_MODEL), D_CONV ** -0.5),
        'g_q': gain(ks[12], (DEPTH, Q_LORA)),
        'w_uq': nrm(ks[13], (DEPTH, Q_LORA, N_HEADS * (QK_NOPE + QK_ROPE)), Q_LORA ** -0.5),
        'g_kv': gain(ks[14], (DEPTH, KV_LORA)),
        'w_ukv': nrm(ks[15], (DEPTH, KV_LORA, N_HEADS * (QK_NOPE + V_HEAD)), KV_LORA ** -0.5),
        'w_mla_out': nrm(ks[16], (DEPTH, N_HEADS * V_HEAD, D_MODEL), (N_HEADS * V_HEAD) ** -0.5),
        'w_mix_out': nrm(ks[17], (DEPTH, D_MODEL, D_MODEL), D_MODEL ** -0.5),
        'g_mem_q': gain(ks[18], (DEPTH, D_MODEL)),
        'g_mem_kv': gain(ks[19], (DEPTH, D_MODEL)),
        'w_qm': nrm(ks[20], (DEPTH, D_MODEL, MEM_HEADS * MEM_HEAD_DIM), D_MODEL ** -0.5),
        'w_km': nrm(ks[21], (DEPTH, D_MODEL, MEM_HEADS * MEM_HEAD_DIM), D_MODEL ** -0.5),
        'w_vm': nrm(ks[22], (DEPTH, D_MODEL, MEM_HEADS * MEM_HEAD_DIM), D_MODEL ** -0.5),
        'w_om': nrm(ks[23], (DEPTH, MEM_HEADS * MEM_HEAD_DIM, D_MODEL), D_MODEL ** -0.5),
        'g_mlp': gain(ks[24], (DEPTH, D_MODEL)),
        'w_up': nrm(ks[25], (DEPTH, D_MODEL, D_FF), D_MODEL ** -0.5),
        'w_down': nrm(ks[26], (DEPTH, D_FF, D_MODEL), D_FF ** -0.5),
        'g_final': gain(ks[27], (D_MODEL,)),
    }


def reference(x_prompt, x_sample, cache_conv, cache_ckv, cache_krope, cache_mem_k, cache_mem_v, mem_prompt,
              g_mix, w_in, w_conv, w_conv_out, g_q, w_uq, g_kv, w_ukv, w_mla_out, w_mix_out,
              g_mem_q, g_mem_kv, w_qm, w_km, w_vm, w_om, g_mlp, w_up, w_down, g_final):
    xp, xs = x_prompt, x_sample
    Bp = xp.shape[0]
    p_conv, p_ckv, p_kr, p_mk, p_mv = [], [], [], [], []
    s_conv, s_ckv, s_kr = [], [], []
    for l in range(DEPTH):
        shared = (g_mix[l], w_in[l], w_conv[l], w_conv_out[l], g_q[l], w_uq[l], g_kv[l], w_ukv[l],
                  w_mla_out[l], w_mix_out[l], g_mem_q[l], w_qm[l], w_om[l], g_mlp[l], w_up[l], w_down[l])
        mk, mv = memory_kv(mem_prompt, g_mem_kv[l], w_km[l], w_vm[l])
        xp, b_p, c_p, r_p = layer(xp,
                                  jnp.zeros((Bp, CONV_WIDTH - 1, D_CONV), xp.dtype),
                                  jnp.zeros((Bp, 0, KV_LORA), xp.dtype),
                                  jnp.zeros((Bp, 0, QK_ROPE), xp.dtype),
                                  mk, mv, True, *shared)
        xs, b_s, c_s, r_s = layer(xs, cache_conv[l], cache_ckv[l], cache_krope[l],
                                  cache_mem_k[l], cache_mem_v[l], False, *shared)
        p_conv.append(b_p); p_ckv.append(c_p); p_kr.append(r_p); p_mk.append(mk); p_mv.append(mv)
        s_conv.append(b_s); s_ckv.append(c_s); s_kr.append(r_s)
    y_prompt = rms_norm(xp, g_final)
    y_sample = rms_norm(xs, g_final)
    return (y_prompt, y_sample,
            jnp.stack(p_conv), jnp.stack(p_ckv), jnp.stack(p_kr), jnp.stack(p_mk), jnp.stack(p_mv),
            jnp.stack(s_conv), jnp.stack(s_ckv), jnp.stack(s_kr))
```

```python
import functools

import jax
import jax.numpy as jnp
from jax import lax
from jax.experimental import pallas as pl
from jax.experimental.pallas import tpu as pltpu

CHUNK = 64
CONV_WIDTH = 3
N_HEADS = 16
QK_NOPE = 64
QK_ROPE = 32
V_HEAD = 64
Q_LORA = 512
KV_LORA = 256
ROPE_THETA = 10000.0
MEM_HEADS = 4
EPS = 1e-6
ATTN_SCALE = (QK_NOPE + QK_ROPE) ** -0.5

LANES = 128
SUBLANES = 8
HEAD_PAD = LANES
HALF_ROPE = QK_ROPE // 2
VMEM_LIMIT = 56 * 1024 * 1024

BF16 = jnp.bfloat16
F32 = jnp.float32


def _dot(a, b):
    return jnp.dot(a, b, preferred_element_type=F32)


def _dot_nt(a, b):
    return lax.dot_general(a, b, (((1,), (1,)), ((), ())), preferred_element_type=F32)


def _rms(x, g):
    return x * lax.rsqrt(jnp.mean(x * x, axis=-1, keepdims=True) + EPS) * g


def _rope(x, cos_t, sin_a, sin_b):
    return (x * cos_t + pltpu.roll(x, HALF_ROPE, 1) * sin_a
            + pltpu.roll(x, LANES - HALF_ROPE, 1) * sin_b)


def _full(shape):
    return pl.BlockSpec(shape, lambda *_: (0,) * len(shape))


def _params(sem):
    return pltpu.CompilerParams(dimension_semantics=sem, vmem_limit_bytes=VMEM_LIMIT)


def _in_proj_kernel(x_ref, cache_ref, gmix_ref, win_ref, wconv_ref, wco_ref, gq_ref, wuq_ref,
                    gkv_ref, qtab_ref, ktab_ref,
                    ga_ref, sgm_ref, q_ref, ckv_ref, kr_ref, krpad_ref, newconv_ref,
                    vbuf_ref, *, nb, tm, d):
    j = pl.program_id(1)
    m = nb * tm
    c_u, c_b, c_c = 0, d, 2 * d
    c_q = 3 * d
    c_kv = c_q + Q_LORA
    c_kr = c_kv + KV_LORA
    c_ac = c_kr + HEAD_PAD
    c_am = c_ac + d

    x = x_ref[...].reshape(m, d)
    n = _rms(x, gmix_ref[...]).astype(BF16)

    u = _dot(n, win_ref[:, c_u:c_u + d])
    gc = _dot(n, win_ref[:, c_c:c_c + d])
    v = gc * u
    w0 = wconv_ref[0:1, :]
    w1 = wconv_ref[1:2, :]
    w2 = wconv_ref[2:3, :]
    ys = []
    for i in range(nb):
        vi = v[i * tm:(i + 1) * tm]

        @pl.when(j == 0)
        def _():
            vbuf_ref[i, SUBLANES - 2:SUBLANES, :] = cache_ref[i]

        vbuf_ref[i, SUBLANES:SUBLANES + tm, :] = vi
        ys.append(vbuf_ref[i, SUBLANES - 2:SUBLANES - 2 + tm, :] * w0
                  + vbuf_ref[i, SUBLANES - 1:SUBLANES - 1 + tm, :] * w1
                  + vi * w2)
        tail = vbuf_ref[i, SUBLANES + tm - 2:SUBLANES + tm, :]
        vbuf_ref[i, SUBLANES - 2:SUBLANES, :] = tail
        newconv_ref[i] = tail
    y = ys[0] if nb == 1 else jnp.concatenate(ys, axis=0)
    gb = _dot(n, win_ref[:, c_b:c_b + d])
    ya = _dot((gb * y).astype(BF16), wco_ref[...])
    ac = _dot(n, win_ref[:, c_ac:c_ac + d])
    ga_ref[...] = (jax.nn.sigmoid(ac) * ya).reshape(nb, tm, d)
    am = _dot(n, win_ref[:, c_am:c_am + d])
    sgm_ref[...] = jax.nn.sigmoid(am).reshape(nb, tm, d)

    cq = _dot(n, win_ref[:, c_q:c_q + Q_LORA])
    cqn = _rms(cq, gq_ref[...]).astype(BF16)
    qc, qa, qb = qtab_ref[0], qtab_ref[1], qtab_ref[2]
    for h in range(N_HEADS):
        qh = _dot(cqn, wuq_ref[:, h * HEAD_PAD:(h + 1) * HEAD_PAD])
        qh = _rope(qh, qc, qa, qb).astype(BF16)
        for i in range(nb):
            q_ref[i, h] = qh[i * tm:(i + 1) * tm]

    ckv = _dot(n, win_ref[:, c_kv:c_kv + KV_LORA])
    ckv_ref[...] = _rms(ckv, gkv_ref[...]).reshape(nb, tm, KV_LORA)
    kr = _dot(n, win_ref[:, c_kr:c_kr + HEAD_PAD])
    kr = _rope(kr, ktab_ref[0], ktab_ref[1], ktab_ref[2])
    kr_ref[...] = kr[:, :QK_ROPE].reshape(nb, tm, QK_ROPE)
    krpad_ref[...] = kr.astype(BF16).reshape(nb, tm, HEAD_PAD)


def _in_proj(x, conv_cache, g_mix, w_in_p, w_conv, w_co, g_q, w_uq_p, g_kv, qtab, ktab, *, nb, tm):
    b, t, d = x.shape
    m = nb * tm
    ncol = w_in_p.shape[1]
    grid = (b // nb, t // tm)
    row = lambda bi, j: (bi, j, 0)
    out_shape = (
        jax.ShapeDtypeStruct((b, t, d), F32),
        jax.ShapeDtypeStruct((b, t, d), F32),
        jax.ShapeDtypeStruct((b, N_HEADS, t, HEAD_PAD), BF16),
        jax.ShapeDtypeStruct((b, t, KV_LORA), F32),
        jax.ShapeDtypeStruct((b, t, QK_ROPE), F32),
        jax.ShapeDtypeStruct((b, t, HEAD_PAD), BF16),
        jax.ShapeDtypeStruct((b, CONV_WIDTH - 1, d), F32),
    )
    in_specs = [
        pl.BlockSpec((nb, tm, d), row),
        pl.BlockSpec((nb, CONV_WIDTH - 1, d), lambda bi, j: (bi, 0, 0)),
        _full((1, d)),
        _full((d, ncol)),
        _full((CONV_WIDTH, d)),
        _full((d, d)),
        _full((1, Q_LORA)),
        _full((Q_LORA, N_HEADS * HEAD_PAD)),
        _full((1, KV_LORA)),
        pl.BlockSpec((3, m, LANES), lambda bi, j: (0, j, 0)),
        pl.BlockSpec((3, m, LANES), lambda bi, j: (0, j, 0)),
    ]
    out_specs = (
        pl.BlockSpec((nb, tm, d), row),
        pl.BlockSpec((nb, tm, d), row),
        pl.BlockSpec((nb, N_HEADS, tm, HEAD_PAD), lambda bi, j: (bi, 0, j, 0)),
        pl.BlockSpec((nb, tm, KV_LORA), row),
        pl.BlockSpec((nb, tm, QK_ROPE), row),
        pl.BlockSpec((nb, tm, HEAD_PAD), row),
        pl.BlockSpec((nb, CONV_WIDTH - 1, d), lambda bi, j: (bi, 0, 0)),
    )
    return pl.pallas_call(
        functools.partial(_in_proj_kernel, nb=nb, tm=tm, d=d),
        grid=grid, in_specs=in_specs, out_specs=out_specs, out_shape=out_shape,
        scratch_shapes=[pltpu.VMEM((nb, tm + SUBLANES, d), F32)],
        compiler_params=_params(("arbitrary", "arbitrary")),
        name="in_proj",
    )(x, conv_cache, g_mix, w_in_p, w_conv, w_co, g_q, w_uq_p, g_kv, qtab, ktab)


def _kv_expand_kernel(ckv_ref, krpad_ref, wuk_ref, wuv_ref, k_ref, v_ref):
    c = ckv_ref[0].astype(BF16)
    kr = krpad_ref[0].astype(F32)
    for h in range(N_HEADS):
        kh = _dot(c, wuk_ref[:, h * HEAD_PAD:(h + 1) * HEAD_PAD]) + kr
        k_ref[0, h] = kh.astype(BF16)
    v_ref[0] = _dot(c, wuv_ref[...]).astype(BF16)


def _kv_expand(ckv, kr_pad, w_uk_p, w_uv, *, tl):
    b, l, _ = ckv.shape
    dv = N_HEADS * V_HEAD
    return pl.pallas_call(
        _kv_expand_kernel,
        grid=(b, l // tl),
        in_specs=[
            pl.BlockSpec((1, tl, KV_LORA), lambda bi, j: (bi, j, 0)),
            pl.BlockSpec((1, tl, HEAD_PAD), lambda bi, j: (bi, j, 0)),
            _full((KV_LORA, N_HEADS * HEAD_PAD)),
            _full((KV_LORA, dv)),
        ],
        out_specs=(
            pl.BlockSpec((1, N_HEADS, tl, HEAD_PAD), lambda bi, j: (bi, 0, j, 0)),
            pl.BlockSpec((1, tl, dv), lambda bi, j: (bi, j, 0)),
        ),
        out_shape=(
            jax.ShapeDtypeStruct((b, N_HEADS, l, HEAD_PAD), BF16),
            jax.ShapeDtypeStruct((b, l, dv), BF16),
        ),
        compiler_params=_params(("arbitrary", "arbitrary")),
        name="kv_expand",
    )(ckv, kr_pad, w_uk_p, w_uv)


def _attention_kernel(q_ref, k_ref, v_ref, o_ref, *, tq, tk, past):
    i = pl.program_id(2)
    q0 = q_ref[0, 0]
    q1 = q_ref[0, 1]
    lane = lax.broadcasted_iota(jnp.int32, (tq, 2 * V_HEAD), 1)
    first = lane < V_HEAD

    def update(carry, k0, k1, vb, mask):
        m0, l0, m1, l1, acc = carry
        outs = []
        for qh, kh, m_old, l_old in ((q0, k0, m0, l0), (q1, k1, m1, l1)):
            s = _dot_nt(qh, kh)
            if mask is not None:
                s = jnp.where(mask, s, -jnp.inf)
            m_new = jnp.maximum(m_old, jnp.max(s, axis=-1, keepdims=True))
            alpha = jnp.exp(m_old - m_new)
            p = jnp.exp(s - m_new)
            l_new = alpha * l_old + jnp.sum(p, axis=-1, keepdims=True)
            pv = _dot(p.astype(BF16), vb)
            outs.append((m_new, l_new, alpha, pv))
        (m0, l0, a0, pv0), (m1, l1, a1, pv1) = outs
        acc = acc * jnp.where(first, a0, a1) + jnp.where(first, pv0, pv1)
        return m0, l0, m1, l1, acc

    def full_block(kb, carry):
        start = pl.multiple_of(kb * tk, tk)
        return update(carry, k_ref[0, 0, pl.ds(start, tk), :], k_ref[0, 1, pl.ds(start, tk), :],
                      v_ref[0, pl.ds(start, tk), :], None)

    neg = jnp.full((tq, 1), -jnp.inf, F32)
    zero = jnp.zeros((tq, 1), F32)
    carry = (neg, zero, neg, zero, jnp.zeros((tq, 2 * V_HEAD), F32))
    n_full = (past + i * tq) // tk
    carry = lax.fori_loop(0, n_full, full_block, carry)
    dstart = pl.multiple_of(past + i * tq, tq)
    rq = lax.broadcasted_iota(jnp.int32, (tq, tq), 0) // CHUNK
    ck = lax.broadcasted_iota(jnp.int32, (tq, tq), 1) // CHUNK
    carry = update(carry, k_ref[0, 0, pl.ds(dstart, tq), :], k_ref[0, 1, pl.ds(dstart, tq), :],
                   v_ref[0, pl.ds(dstart, tq), :], ck <= rq)
    _, l0, _, l1, acc = carry
    o_ref[0] = (acc * jnp.where(first, 1.0 / l0, 1.0 / l1)).astype(o_ref.dtype)


def _attention(q, k, v, *, tq, tk):
    b, _, t, _ = q.shape
    l = k.shape[2]
    past = l - t
    assert tq % CHUNK == 0 and all((past + i * tq) % tk == 0 for i in range(t // tq))
    return pl.pallas_call(
        functools.partial(_attention_kernel, tq=tq, tk=tk, past=past),
        grid=(b, N_HEADS // 2, t // tq),
        in_specs=[
            pl.BlockSpec((1, 2, tq, HEAD_PAD), lambda bi, hp, i: (bi, hp, i, 0)),
            pl.BlockSpec((1, 2, l, HEAD_PAD), lambda bi, hp, i: (bi, hp, 0, 0)),
            pl.BlockSpec((1, l, 2 * V_HEAD), lambda bi, hp, i: (bi, 0, hp)),
        ],
        out_specs=pl.BlockSpec((1, tq, 2 * V_HEAD), lambda bi, hp, i: (bi, i, hp)),
        out_shape=jax.ShapeDtypeStruct((b, t, N_HEADS * V_HEAD), BF16),
        compiler_params=_params(("arbitrary", "arbitrary", "arbitrary")),
        name="attention",
    )(q, k, v)


def _post_attn_kernel(x_ref, o_ref, ga_ref, sgm_ref, wmo_ref, wmix_ref, gmq_ref, wqm_ref,
                      mk_ref, mv_ref, wom_ref, x2_ref, om_ref, *, nb, tm, d):
    m = nb * tm
    hd = d // MEM_HEADS
    mem_scale = hd ** -0.5
    x = x_ref[...].reshape(m, d)
    yb = _dot(o_ref[...].reshape(m, d), wmo_ref[...])
    mix = ga_ref[...].reshape(m, d) + sgm_ref[...].reshape(m, d) * yb
    x1 = x + _dot(mix.astype(BF16), wmix_ref[...])
    hq = _rms(x1, gmq_ref[...]).astype(BF16)
    qm = (_dot(hq, wqm_ref[...]) * mem_scale).astype(BF16)
    for i in range(nb):
        mk = mk_ref[i].astype(BF16)
        mv = mv_ref[i].astype(BF16)
        for h in range(MEM_HEADS):
            cols = slice(h * hd, (h + 1) * hd)
            s = _dot_nt(qm[i * tm:(i + 1) * tm, cols], mk[:, cols])
            p = jnp.exp(s - jnp.max(s, axis=-1, keepdims=True))
            l = jnp.sum(p, axis=-1, keepdims=True)
            oh = _dot(p.astype(BF16), mv[:, cols]) / l
            om_ref[i * tm:(i + 1) * tm, cols] = oh.astype(BF16)
    x2 = x1 + _dot(om_ref[...], wom_ref[...])
    x2_ref[...] = x2.reshape(nb, tm, d)


def _post_attn(x, o, ga, sgm, w_mo, w_mix, g_mq, w_qm, mem_k, mem_v, w_om, *, nb, tm):
    b, t, d = x.shape
    n_mem = mem_k.shape[1]
    row = lambda bi, j: (bi, j, 0)
    per_b = lambda bi, j: (bi, 0, 0)
    return pl.pallas_call(
        functools.partial(_post_attn_kernel, nb=nb, tm=tm, d=d),
        grid=(b // nb, t // tm),
        in_specs=[
            pl.BlockSpec((nb, tm, d), row),
            pl.BlockSpec((nb, tm, d), row),
            pl.BlockSpec((nb, tm, d), row),
            pl.BlockSpec((nb, tm, d), row),
            _full((d, d)), _full((d, d)), _full((1, d)), _full((d, d)),
            pl.BlockSpec((nb, n_mem, d), per_b),
            pl.BlockSpec((nb, n_mem, d), per_b),
            _full((d, d)),
        ],
        out_specs=pl.BlockSpec((nb, tm, d), row),
        out_shape=jax.ShapeDtypeStruct((b, t, d), F32),
        scratch_shapes=[pltpu.VMEM((nb * tm, d), BF16)],
        compiler_params=_params(("arbitrary", "arbitrary")),
        name="post_attn",
    )(x, o, ga, sgm, w_mo, w_mix, g_mq, w_qm, mem_k, mem_v, w_om)


def _mlp_kernel(x_ref, gmlp_ref, wup_ref, wdown_ref, gfin_ref, y_ref, *, fchunk):
    x = x_ref[...]
    hm = _rms(x, gmlp_ref[...]).astype(BF16)
    acc = x
    for c in range(wup_ref.shape[1] // fchunk):
        cols = slice(c * fchunk, (c + 1) * fchunk)
        hcol = jnp.maximum(_dot(hm, wup_ref[:, cols]), 0.0)
        acc = acc + _dot((hcol * hcol).astype(BF16), wdown_ref[cols, :])
    y_ref[...] = _rms(acc, gfin_ref[...])


def _mlp(x, g_mlp, w_up, w_down, g_final, *, tm):
    n, d = x.shape
    f = w_up.shape[1]
    return pl.pallas_call(
        functools.partial(_mlp_kernel, fchunk=min(f, 1024)),
        grid=(n // tm,),
        in_specs=[
            pl.BlockSpec((tm, d), lambda r: (r, 0)),
            _full((1, d)), _full((d, f)), _full((f, d)), _full((1, d)),
        ],
        out_specs=pl.BlockSpec((tm, d), lambda r: (r, 0)),
        out_shape=jax.ShapeDtypeStruct((n, d), F32),
        compiler_params=_params(("arbitrary",)),
        name="mlp",
    )(x, g_mlp, w_up, w_down, g_final)


def _mem_kv_kernel(mem_ref, g_ref, wk_ref, wv_ref, k_ref, v_ref):
    mn = _rms(mem_ref[0], g_ref[...]).astype(BF16)
    k_ref[0] = _dot(mn, wk_ref[...])
    v_ref[0] = _dot(mn, wv_ref[...])


def _mem_kv(mem, g, w_k, w_v):
    b, n_mem, d = mem.shape
    blk = pl.BlockSpec((1, n_mem, d), lambda bi: (bi, 0, 0))
    return pl.pallas_call(
        _mem_kv_kernel,
        grid=(b,),
        in_specs=[blk, _full((1, d)), _full((d, d)), _full((d, d))],
        out_specs=(blk, blk),
        out_shape=(jax.ShapeDtypeStruct((b, n_mem, d), F32),) * 2,
        compiler_params=_params(("arbitrary",)),
        name="mem_kv",
    )(mem, g, w_k, w_v)


def _rope_tables(pos, scale):
    t = pos.shape[0]
    inv = ROPE_THETA ** (-jnp.arange(HALF_ROPE, dtype=F32) / HALF_ROPE)
    ang = pos.astype(F32)[:, None] * inv[None, :]
    cos, sin = jnp.cos(ang), jnp.sin(ang)
    z = lambda w: jnp.zeros((t, w), F32)
    cos_t = jnp.concatenate([cos, cos, jnp.ones((t, LANES - QK_ROPE), F32)], axis=1)
    sin_a = jnp.concatenate([z(HALF_ROPE), sin, z(LANES - QK_ROPE)], axis=1)
    sin_b = jnp.concatenate([-sin, z(LANES - HALF_ROPE)], axis=1)
    return jnp.stack([cos_t, sin_a, sin_b]) * scale


def _prep_weights(w_in, w_conv_out, w_uq, w_ukv, w_mla_out, w_mix_out, w_qm, w_om, w_up, w_down):
    d = w_in.shape[0]
    c_kr = 3 * d + Q_LORA + KV_LORA
    w_in_p = jnp.concatenate(
        [w_in[:, :c_kr], w_in[:, c_kr:c_kr + QK_ROPE],
         jnp.zeros((d, HEAD_PAD - QK_ROPE), w_in.dtype), w_in[:, c_kr + QK_ROPE:]], axis=1)
    pad = HEAD_PAD - QK_ROPE - QK_NOPE
    wq = w_uq.reshape(Q_LORA, N_HEADS, QK_NOPE + QK_ROPE)
    w_uq_p = jnp.concatenate(
        [wq[..., QK_NOPE:], wq[..., :QK_NOPE], jnp.zeros((Q_LORA, N_HEADS, pad), w_uq.dtype)],
        axis=-1).reshape(Q_LORA, N_HEADS * HEAD_PAD)
    wkv = w_ukv.reshape(KV_LORA, N_HEADS, QK_NOPE + V_HEAD)
    w_uk_p = jnp.concatenate(
        [jnp.zeros((KV_LORA, N_HEADS, QK_ROPE), w_ukv.dtype), wkv[..., :QK_NOPE],
         jnp.zeros((KV_LORA, N_HEADS, pad), w_ukv.dtype)], axis=-1).reshape(KV_LORA, N_HEADS * HEAD_PAD)
    w_uv = wkv[..., QK_NOPE:].reshape(KV_LORA, N_HEADS * V_HEAD)
    cast = lambda w: w.astype(BF16)
    return tuple(map(cast, (w_in_p, w_conv_out, w_uq_p, w_uk_p, w_uv, w_mla_out, w_mix_out,
                            w_qm, w_om, w_up, w_down)))


def _layer(x, conv_cache, past_ckv, past_kr, mem_k, mem_v, weights, gains, w_conv, g_final,
           *, nb, tm, tl, tq, tk, tm_mlp):
    (w_in_p, w_co, w_uq_p, w_uk_p, w_uv, w_mo, w_mix, w_qm, w_om, w_up, w_down) = weights
    g_mix, g_q, g_kv, g_mem_q, g_mlp = gains
    b, t, d = x.shape
    past = past_ckv.shape[1]
    pos = past + jnp.arange(t, dtype=jnp.int32)
    qtab = jnp.tile(_rope_tables(pos, ATTN_SCALE), (1, nb, 1))
    ktab = jnp.tile(_rope_tables(pos, 1.0), (1, nb, 1))
    ga, sgm, q, ckv, kr, kr_pad, new_conv = _in_proj(
        x, conv_cache, g_mix, w_in_p, w_conv, w_co, g_q, w_uq_p, g_kv, qtab, ktab, nb=nb, tm=tm)
    if past:
        past_kr_pad = jnp.pad(past_kr, ((0, 0), (0, 0), (0, HEAD_PAD - QK_ROPE))).astype(BF16)
        ckv_all = jnp.concatenate([past_ckv, ckv], axis=1)
        kr_all = jnp.concatenate([past_kr_pad, kr_pad], axis=1)
    else:
        ckv_all, kr_all = ckv, kr_pad
    k, v = _kv_expand(ckv_all, kr_all, w_uk_p, w_uv, tl=tl)
    o = _attention(q, k, v, tq=tq, tk=tk)
    x2 = _post_attn(x, o, ga, sgm, w_mo, w_mix, g_mem_q, w_qm, mem_k, mem_v, w_om, nb=nb, tm=tm)
    y = _mlp(x2.reshape(b * t, d), g_mlp, w_up, w_down, g_final, tm=tm_mlp).reshape(b, t, d)
    return y, new_conv, ckv, kr


def kernel(x_prompt, x_sample, cache_conv, cache_ckv, cache_krope, cache_mem_k, cache_mem_v, mem_prompt, g_mix, w_in, w_conv, w_conv_out, g_q, w_uq, g_kv, w_ukv, w_mla_out, w_mix_out, g_mem_q, g_mem_kv, w_qm, w_km, w_vm, w_om, g_mlp, w_up, w_down, g_final):
    depth = w_in.shape[0]
    assert depth == 1, "single-layer step"
    bp, tp, d = x_prompt.shape
    bs, ts, _ = x_sample.shape
    n_mem = mem_prompt.shape[1]
    row = lambda g: g.reshape(1, -1)
    weights = _prep_weights(w_in[0], w_conv_out[0], w_uq[0], w_ukv[0], w_mla_out[0], w_mix_out[0],
                            w_qm[0], w_om[0], w_up[0], w_down[0])
    gains = (row(g_mix[0]), row(g_q[0]), row(g_kv[0]), row(g_mem_q[0]), row(g_mlp[0]))
    g_fin = row(g_final)

    mk, mv = _mem_kv(mem_prompt, row(g_mem_kv[0]), w_km[0].astype(BF16), w_vm[0].astype(BF16))
    y_p, conv_p, ckv_p, kr_p = _layer(
        x_prompt, jnp.zeros((bp, CONV_WIDTH - 1, d), F32), jnp.zeros((bp, 0, KV_LORA), F32),
        jnp.zeros((bp, 0, QK_ROPE), F32), mk, mv, weights, gains, w_conv[0], g_fin,
        nb=1, tm=256, tl=512, tq=256, tk=256, tm_mlp=512)
    past = cache_ckv.shape[2]
    y_s, conv_s, ckv_s, kr_s = _layer(
        x_sample, cache_conv[0], cache_ckv[0], cache_krope[0],
        cache_mem_k[0].reshape(bs, n_mem, d), cache_mem_v[0].reshape(bs, n_mem, d),
        weights, gains, w_conv[0], g_fin,
        nb=bs, tm=ts, tl=past + ts, tq=ts, tk=past, tm_mlp=bs * ts)
    mshape = (1, bp, n_mem, MEM_HEADS, d // MEM_HEADS)
    return (y_p, y_s, conv_p[None], ckv_p[None], kr_p[None], mk.reshape(mshape), mv.reshape(mshape),
            conv_s[None], ckv_s[None], kr_s[None])
```

```python
import functools

import jax
import jax.numpy as jnp
from jax import lax
from jax.experimental import pallas as pl
from jax.experimental.pallas import tpu as pltpu

CHUNK = 64
CONV_WIDTH = 3
N_HEADS = 16
QK_NOPE = 64
QK_ROPE = 32
V_HEAD = 64
Q_LORA = 512
KV_LORA = 256
ROPE_THETA = 10000.0
MEM_HEADS = 4
EPS = 1e-6
ATTN_SCALE = (QK_NOPE + QK_ROPE) ** -0.5
LOG2_E = 1.4426950408889634

LANES = 128
SUBLANES = 8
HEAD_PAD = LANES
HALF_ROPE = QK_ROPE // 2
VMEM_LIMIT = 56 * 1024 * 1024

BF16 = jnp.bfloat16
F32 = jnp.float32


def _dot(a, b):
    return jnp.dot(a, b, preferred_element_type=F32)


def _dot_nt(a, b):
    return lax.dot_general(a, b, (((1,), (1,)), ((), ())), preferred_element_type=F32)


def _rms(x, g):
    return x * lax.rsqrt(jnp.mean(x * x, axis=-1, keepdims=True) + EPS) * g


def _rope(x, cos_t, sin_a, sin_b):
    return (x * cos_t + pltpu.roll(x, HALF_ROPE, 1) * sin_a
            + pltpu.roll(x, LANES - HALF_ROPE, 1) * sin_b)


def _full(shape):
    return pl.BlockSpec(shape, lambda *_: (0,) * len(shape))


def _params(sem):
    return pltpu.CompilerParams(dimension_semantics=sem, vmem_limit_bytes=VMEM_LIMIT)


def _in_proj_kernel(x_ref, cache_ref, gmix_ref, win_ref, wconv_ref, wco_ref, gq_ref, wuq_ref,
                    gkv_ref, qtab_ref, ktab_ref,
                    ga_ref, sgm_ref, q_ref, ckv_ref, kr_ref, krpad_ref, newconv_ref,
                    vbuf_ref, *, nb, tm, d):
    j = pl.program_id(1)
    m = nb * tm
    c_u, c_b, c_c = 0, d, 2 * d
    c_q = 3 * d
    c_kv = c_q + Q_LORA
    c_kr = c_kv + KV_LORA
    c_ac = c_kr + HEAD_PAD
    c_am = c_ac + d

    x = x_ref[...].reshape(m, d)
    n = _rms(x, gmix_ref[...]).astype(BF16)

    u = _dot(n, win_ref[:, c_u:c_u + d])
    gc = _dot(n, win_ref[:, c_c:c_c + d])
    v = gc * u
    w0 = wconv_ref[0:1, :]
    w1 = wconv_ref[1:2, :]
    w2 = wconv_ref[2:3, :]
    ys = []
    for i in range(nb):
        vi = v[i * tm:(i + 1) * tm]

        @pl.when(j == 0)
        def _():
            vbuf_ref[i, SUBLANES - 2:SUBLANES, :] = cache_ref[i]

        vbuf_ref[i, SUBLANES:SUBLANES + tm, :] = vi
        ys.append(vbuf_ref[i, SUBLANES - 2:SUBLANES - 2 + tm, :] * w0
                  + vbuf_ref[i, SUBLANES - 1:SUBLANES - 1 + tm, :] * w1
                  + vi * w2)
        tail = vbuf_ref[i, SUBLANES + tm - 2:SUBLANES + tm, :]
        vbuf_ref[i, SUBLANES - 2:SUBLANES, :] = tail
        newconv_ref[i] = tail
    y = ys[0] if nb == 1 else jnp.concatenate(ys, axis=0)
    gb = _dot(n, win_ref[:, c_b:c_b + d])
    ya = _dot((gb * y).astype(BF16), wco_ref[...])
    ac = _dot(n, win_ref[:, c_ac:c_ac + d])
    ga_ref[...] = (jax.nn.sigmoid(ac) * ya).reshape(nb, tm, d)
    am = _dot(n, win_ref[:, c_am:c_am + d])
    sgm_ref[...] = jax.nn.sigmoid(am).reshape(nb, tm, d)

    cq = _dot(n, win_ref[:, c_q:c_q + Q_LORA])
    cqn = _rms(cq, gq_ref[...]).astype(BF16)
    qc, qa, qb = qtab_ref[0], qtab_ref[1], qtab_ref[2]
    for h in range(N_HEADS):
        qh = _dot(cqn, wuq_ref[:, h * HEAD_PAD:(h + 1) * HEAD_PAD])
        qh = _rope(qh, qc, qa, qb).astype(BF16)
        for i in range(nb):
            q_ref[i, h] = qh[i * tm:(i + 1) * tm]

    ckv = _dot(n, win_ref[:, c_kv:c_kv + KV_LORA])
    ckv_ref[...] = _rms(ckv, gkv_ref[...]).reshape(nb, tm, KV_LORA)
    kr = _dot(n, win_ref[:, c_kr:c_kr + HEAD_PAD])
    kr = _rope(kr, ktab_ref[0], ktab_ref[1], ktab_ref[2])
    kr_ref[...] = kr[:, :QK_ROPE].reshape(nb, tm, QK_ROPE)
    krpad_ref[...] = kr.astype(BF16).reshape(nb, tm, HEAD_PAD)


def _in_proj(x, conv_cache, g_mix, w_in_p, w_conv, w_co, g_q, w_uq_p, g_kv, qtab, ktab, *, nb, tm):
    b, t, d = x.shape
    m = nb * tm
    ncol = w_in_p.shape[1]
    grid = (b // nb, t // tm)
    row = lambda bi, j: (bi, j, 0)
    out_shape = (
        jax.ShapeDtypeStruct((b, t, d), F32),
        jax.ShapeDtypeStruct((b, t, d), F32),
        jax.ShapeDtypeStruct((b, N_HEADS, t, HEAD_PAD), BF16),
        jax.ShapeDtypeStruct((b, t, KV_LORA), F32),
        jax.ShapeDtypeStruct((b, t, QK_ROPE), F32),
        jax.ShapeDtypeStruct((b, t, HEAD_PAD), BF16),
        jax.ShapeDtypeStruct((b, CONV_WIDTH - 1, d), F32),
    )
    in_specs = [
        pl.BlockSpec((nb, tm, d), row),
        pl.BlockSpec((nb, CONV_WIDTH - 1, d), lambda bi, j: (bi, 0, 0)),
        _full((1, d)),
        _full((d, ncol)),
        _full((CONV_WIDTH, d)),
        _full((d, d)),
        _full((1, Q_LORA)),
        _full((Q_LORA, N_HEADS * HEAD_PAD)),
        _full((1, KV_LORA)),
        pl.BlockSpec((3, m, LANES), lambda bi, j: (0, j, 0)),
        pl.BlockSpec((3, m, LANES), lambda bi, j: (0, j, 0)),
    ]
    out_specs = (
        pl.BlockSpec((nb, tm, d), row),
        pl.BlockSpec((nb, tm, d), row),
        pl.BlockSpec((nb, N_HEADS, tm, HEAD_PAD), lambda bi, j: (bi, 0, j, 0)),
        pl.BlockSpec((nb, tm, KV_LORA), row),
        pl.BlockSpec((nb, tm, QK_ROPE), row),
        pl.BlockSpec((nb, tm, HEAD_PAD), row),
        pl.BlockSpec((nb, CONV_WIDTH - 1, d), lambda bi, j: (bi, 0, 0)),
    )
    return pl.pallas_call(
        functools.partial(_in_proj_kernel, nb=nb, tm=tm, d=d),
        grid=grid, in_specs=in_specs, out_specs=out_specs, out_shape=out_shape,
        scratch_shapes=[pltpu.VMEM((nb, tm + SUBLANES, d), F32)],
        compiler_params=_params(("arbitrary", "arbitrary")),
        name="in_proj",
    )(x, conv_cache, g_mix, w_in_p, w_conv, w_co, g_q, w_uq_p, g_kv, qtab, ktab)


def _kv_expand_kernel(ckv_ref, krpad_ref, wuk_ref, wuv_ref, k_ref, v_ref):
    c = ckv_ref[0].astype(BF16)
    kr = krpad_ref[0].astype(F32)
    ones = (lax.broadcasted_iota(jnp.int32, kr.shape, 1) >= V_HEAD).astype(F32)
    for h in range(N_HEADS):
        cols = slice(h * HEAD_PAD, (h + 1) * HEAD_PAD)
        k_ref[0, h] = (_dot(c, wuk_ref[:, cols]) + kr).astype(BF16)
        v_ref[0, h] = (_dot(c, wuv_ref[:, cols]) + ones).astype(BF16)


def _kv_expand(ckv, kr_pad, w_uk_p, w_uv_p, *, tl):
    b, l, _ = ckv.shape
    head_tiles = pl.BlockSpec((1, N_HEADS, tl, HEAD_PAD), lambda bi, j: (bi, 0, j, 0))
    return pl.pallas_call(
        _kv_expand_kernel,
        grid=(b, l // tl),
        in_specs=[
            pl.BlockSpec((1, tl, KV_LORA), lambda bi, j: (bi, j, 0)),
            pl.BlockSpec((1, tl, HEAD_PAD), lambda bi, j: (bi, j, 0)),
            _full((KV_LORA, N_HEADS * HEAD_PAD)),
            _full((KV_LORA, N_HEADS * HEAD_PAD)),
        ],
        out_specs=(head_tiles, head_tiles),
        out_shape=(jax.ShapeDtypeStruct((b, N_HEADS, l, HEAD_PAD), BF16),) * 2,
        compiler_params=_params(("arbitrary", "arbitrary")),
        name="kv_expand",
    )(ckv, kr_pad, w_uk_p, w_uv_p)


def _attention_kernel(q_ref, k_ref, v_ref, o_ref, *, tq, tk, past):
    i = pl.program_id(2)
    qs = (q_ref[0, 0], q_ref[0, 1])

    def update(carry, start, width, mask):
        new = []
        for e in range(2):
            m_old, acc = carry[e]
            s = _dot_nt(qs[e], k_ref[0, e, pl.ds(start, width), :])
            if mask is not None:
                s = jnp.where(mask, s, -jnp.inf)
            m_new = jnp.maximum(m_old, jnp.max(s, axis=-1, keepdims=True))
            alpha = jnp.exp2(m_old - m_new)
            p = jnp.exp2(s - m_new).astype(BF16)
            acc = alpha * acc + _dot(p, v_ref[0, e, pl.ds(start, width), :])
            new.append((m_new, acc))
        return tuple(new)

    def full_block(kb, carry):
        return update(carry, pl.multiple_of(kb * tk, tk), tk, None)

    init = (jnp.full((tq, 1), -jnp.inf, F32), jnp.zeros((tq, HEAD_PAD), F32))
    n_full = (past + i * tq) // tk
    carry = lax.fori_loop(0, n_full, full_block, (init, init))
    rq = lax.broadcasted_iota(jnp.int32, (tq, tq), 0) // CHUNK
    ck = lax.broadcasted_iota(jnp.int32, (tq, tq), 1) // CHUNK
    (_, acc0), (_, acc1) = update(carry, pl.multiple_of(past + i * tq, tq), tq, ck <= rq)
    first = lax.broadcasted_iota(jnp.int32, (tq, HEAD_PAD), 1) < V_HEAD
    num = jnp.where(first, acc0, pltpu.roll(acc1, V_HEAD, 1))
    den = jnp.where(first, pltpu.roll(acc0, V_HEAD, 1), acc1)
    o_ref[0] = (num / den).astype(o_ref.dtype)


def _attention(q, k, v, *, tq, tk):
    b, _, t, _ = q.shape
    l = k.shape[2]
    past = l - t
    assert tq % CHUNK == 0 and all((past + i * tq) % tk == 0 for i in range(t // tq))
    return pl.pallas_call(
        functools.partial(_attention_kernel, tq=tq, tk=tk, past=past),
        grid=(b, N_HEADS // 2, t // tq),
        in_specs=[
            pl.BlockSpec((1, 2, tq, HEAD_PAD), lambda bi, hp, i: (bi, hp, i, 0)),
            pl.BlockSpec((1, 2, l, HEAD_PAD), lambda bi, hp, i: (bi, hp, 0, 0)),
            pl.BlockSpec((1, 2, l, HEAD_PAD), lambda bi, hp, i: (bi, hp, 0, 0)),
        ],
        out_specs=pl.BlockSpec((1, tq, 2 * V_HEAD), lambda bi, hp, i: (bi, i, hp)),
        out_shape=jax.ShapeDtypeStruct((b, t, N_HEADS * V_HEAD), BF16),
        compiler_params=_params(("arbitrary", "arbitrary", "arbitrary")),
        name="attention",
    )(q, k, v)


def _post_attn_kernel(x_ref, o_ref, ga_ref, sgm_ref, wmo_ref, wmix_ref, gmq_ref, wqm_ref,
                      mk_ref, mv_ref, wom_ref, x2_ref, om_ref, *, nb, tm, d):
    m = nb * tm
    hd = d // MEM_HEADS
    mem_scale = hd ** -0.5
    x = x_ref[...].reshape(m, d)
    yb = _dot(o_ref[...].reshape(m, d), wmo_ref[...])
    mix = ga_ref[...].reshape(m, d) + sgm_ref[...].reshape(m, d) * yb
    x1 = x + _dot(mix.astype(BF16), wmix_ref[...])
    hq = _rms(x1, gmq_ref[...]).astype(BF16)
    qm = (_dot(hq, wqm_ref[...]) * mem_scale).astype(BF16)
    for i in range(nb):
        mk = mk_ref[i].astype(BF16)
        mv = mv_ref[i].astype(BF16)
        for h in range(MEM_HEADS):
            cols = slice(h * hd, (h + 1) * hd)
            s = _dot_nt(qm[i * tm:(i + 1) * tm, cols], mk[:, cols])
            p = jnp.exp(s - jnp.max(s, axis=-1, keepdims=True))
            l = jnp.sum(p, axis=-1, keepdims=True)
            oh = _dot(p.astype(BF16), mv[:, cols]) / l
            om_ref[i * tm:(i + 1) * tm, cols] = oh.astype(BF16)
    x2 = x1 + _dot(om_ref[...], wom_ref[...])
    x2_ref[...] = x2.reshape(nb, tm, d)


def _post_attn(x, o, ga, sgm, w_mo, w_mix, g_mq, w_qm, mem_k, mem_v, w_om, *, nb, tm):
    b, t, d = x.shape
    n_mem = mem_k.shape[1]
    row = lambda bi, j: (bi, j, 0)
    per_b = lambda bi, j: (bi, 0, 0)
    return pl.pallas_call(
        functools.partial(_post_attn_kernel, nb=nb, tm=tm, d=d),
        grid=(b // nb, t // tm),
        in_specs=[
            pl.BlockSpec((nb, tm, d), row),
            pl.BlockSpec((nb, tm, d), row),
            pl.BlockSpec((nb, tm, d), row),
            pl.BlockSpec((nb, tm, d), row),
            _full((d, d)), _full((d, d)), _full((1, d)), _full((d, d)),
            pl.BlockSpec((nb, n_mem, d), per_b),
            pl.BlockSpec((nb, n_mem, d), per_b),
            _full((d, d)),
        ],
        out_specs=pl.BlockSpec((nb, tm, d), row),
        out_shape=jax.ShapeDtypeStruct((b, t, d), F32),
        scratch_shapes=[pltpu.VMEM((nb * tm, d), BF16)],
        compiler_params=_params(("arbitrary", "arbitrary")),
        name="post_attn",
    )(x, o, ga, sgm, w_mo, w_mix, g_mq, w_qm, mem_k, mem_v, w_om)


def _mlp_kernel(x_ref, gmlp_ref, wup_ref, wdown_ref, gfin_ref, y_ref, *, fchunk):
    x = x_ref[...]
    hm = _rms(x, gmlp_ref[...]).astype(BF16)
    acc = x
    for c in range(wup_ref.shape[1] // fchunk):
        cols = slice(c * fchunk, (c + 1) * fchunk)
        hcol = jnp.maximum(_dot(hm, wup_ref[:, cols]), 0.0)
        acc = acc + _dot((hcol * hcol).astype(BF16), wdown_ref[cols, :])
    y_ref[...] = _rms(acc, gfin_ref[...])


def _mlp(x, g_mlp, w_up, w_down, g_final, *, tm):
    n, d = x.shape
    f = w_up.shape[1]
    return pl.pallas_call(
        functools.partial(_mlp_kernel, fchunk=min(f, 1024)),
        grid=(n // tm,),
        in_specs=[
            pl.BlockSpec((tm, d), lambda r: (r, 0)),
            _full((1, d)), _full((d, f)), _full((f, d)), _full((1, d)),
        ],
        out_specs=pl.BlockSpec((tm, d), lambda r: (r, 0)),
        out_shape=jax.ShapeDtypeStruct((n, d), F32),
        compiler_params=_params(("arbitrary",)),
        name="mlp",
    )(x, g_mlp, w_up, w_down, g_final)


def _mem_kv_kernel(mem_ref, g_ref, wk_ref, wv_ref, k_ref, v_ref):
    mn = _rms(mem_ref[0], g_ref[...]).astype(BF16)
    k_ref[0] = _dot(mn, wk_ref[...])
    v_ref[0] = _dot(mn, wv_ref[...])


def _mem_kv(mem, g, w_k, w_v):
    b, n_mem, d = mem.shape
    blk = pl.BlockSpec((1, n_mem, d), lambda bi: (bi, 0, 0))
    return pl.pallas_call(
        _mem_kv_kernel,
        grid=(b,),
        in_specs=[blk, _full((1, d)), _full((d, d)), _full((d, d))],
        out_specs=(blk, blk),
        out_shape=(jax.ShapeDtypeStruct((b, n_mem, d), F32),) * 2,
        compiler_params=_params(("arbitrary",)),
        name="mem_kv",
    )(mem, g, w_k, w_v)


def _rope_tables(pos, scale):
    t = pos.shape[0]
    inv = ROPE_THETA ** (-jnp.arange(HALF_ROPE, dtype=F32) / HALF_ROPE)
    ang = pos.astype(F32)[:, None] * inv[None, :]
    cos, sin = jnp.cos(ang), jnp.sin(ang)
    z = lambda w: jnp.zeros((t, w), F32)
    cos_t = jnp.concatenate([cos, cos, jnp.ones((t, LANES - QK_ROPE), F32)], axis=1)
    sin_a = jnp.concatenate([z(HALF_ROPE), sin, z(LANES - QK_ROPE)], axis=1)
    sin_b = jnp.concatenate([-sin, z(LANES - HALF_ROPE)], axis=1)
    return jnp.stack([cos_t, sin_a, sin_b]) * scale


def _prep_weights(w_in, w_conv_out, w_uq, w_ukv, w_mla_out, w_mix_out, w_qm, w_om, w_up, w_down):
    d = w_in.shape[0]
    c_kr = 3 * d + Q_LORA + KV_LORA
    w_in_p = jnp.concatenate(
        [w_in[:, :c_kr], w_in[:, c_kr:c_kr + QK_ROPE],
         jnp.zeros((d, HEAD_PAD - QK_ROPE), w_in.dtype), w_in[:, c_kr + QK_ROPE:]], axis=1)
    pad = HEAD_PAD - QK_ROPE - QK_NOPE
    wq = w_uq.reshape(Q_LORA, N_HEADS, QK_NOPE + QK_ROPE)
    w_uq_p = jnp.concatenate(
        [wq[..., QK_NOPE:], wq[..., :QK_NOPE], jnp.zeros((Q_LORA, N_HEADS, pad), w_uq.dtype)],
        axis=-1).reshape(Q_LORA, N_HEADS * HEAD_PAD)
    wkv = w_ukv.reshape(KV_LORA, N_HEADS, QK_NOPE + V_HEAD)
    w_uk_p = jnp.concatenate(
        [jnp.zeros((KV_LORA, N_HEADS, QK_ROPE), w_ukv.dtype), wkv[..., :QK_NOPE],
         jnp.zeros((KV_LORA, N_HEADS, pad), w_ukv.dtype)], axis=-1).reshape(KV_LORA, N_HEADS * HEAD_PAD)
    w_uv_p = jnp.concatenate(
        [wkv[..., QK_NOPE:], jnp.zeros((KV_LORA, N_HEADS, HEAD_PAD - V_HEAD), w_ukv.dtype)],
        axis=-1).reshape(KV_LORA, N_HEADS * HEAD_PAD)
    cast = lambda w: w.astype(BF16)
    return tuple(map(cast, (w_in_p, w_conv_out, w_uq_p, w_uk_p, w_uv_p, w_mla_out, w_mix_out,
                            w_qm, w_om, w_up, w_down)))


def _layer(x, conv_cache, past_ckv, past_kr, mem_k, mem_v, weights, gains, w_conv, g_final,
           *, nb, tm, tl, tq, tk, tm_mlp):
    (w_in_p, w_co, w_uq_p, w_uk_p, w_uv, w_mo, w_mix, w_qm, w_om, w_up, w_down) = weights
    g_mix, g_q, g_kv, g_mem_q, g_mlp = gains
    b, t, d = x.shape
    past = past_ckv.shape[1]
    pos = past + jnp.arange(t, dtype=jnp.int32)
    qtab = jnp.tile(_rope_tables(pos, ATTN_SCALE * LOG2_E), (1, nb, 1))
    ktab = jnp.tile(_rope_tables(pos, 1.0), (1, nb, 1))
    ga, sgm, q, ckv, kr, kr_pad, new_conv = _in_proj(
        x, conv_cache, g_mix, w_in_p, w_conv, w_co, g_q, w_uq_p, g_kv, qtab, ktab, nb=nb, tm=tm)
    if past:
        past_kr_pad = jnp.pad(past_kr, ((0, 0), (0, 0), (0, HEAD_PAD - QK_ROPE))).astype(BF16)
        ckv_all = jnp.concatenate([past_ckv, ckv], axis=1)
        kr_all = jnp.concatenate([past_kr_pad, kr_pad], axis=1)
    else:
        ckv_all, kr_all = ckv, kr_pad
    k, v = _kv_expand(ckv_all, kr_all, w_uk_p, w_uv, tl=tl)
    o = _attention(q, k, v, tq=tq, tk=tk)
    x2 = _post_attn(x, o, ga, sgm, w_mo, w_mix, g_mem_q, w_qm, mem_k, mem_v, w_om, nb=nb, tm=tm)
    y = _mlp(x2.reshape(b * t, d), g_mlp, w_up, w_down, g_final, tm=tm_mlp).reshape(b, t, d)
    return y, new_conv, ckv, kr


def kernel(x_prompt, x_sample, cache_conv, cache_ckv, cache_krope, cache_mem_k, cache_mem_v, mem_prompt, g_mix, w_in, w_conv, w_conv_out, g_q, w_uq, g_kv, w_ukv, w_mla_out, w_mix_out, g_mem_q, g_mem_kv, w_qm, w_km, w_vm, w_om, g_mlp, w_up, w_down, g_final):
    depth = w_in.shape[0]
    assert depth == 1, "single-layer step"
    bp, tp, d = x_prompt.shape
    bs, ts, _ = x_sample.shape
    n_mem = mem_prompt.shape[1]
    row = lambda g: g.reshape(1, -1)
    weights = _prep_weights(w_in[0], w_conv_out[0], w_uq[0], w_ukv[0], w_mla_out[0], w_mix_out[0],
                            w_qm[0], w_om[0], w_up[0], w_down[0])
    gains = (row(g_mix[0]), row(g_q[0]), row(g_kv[0]), row(g_mem_q[0]), row(g_mlp[0]))
    g_fin = row(g_final)

    mk, mv = _mem_kv(mem_prompt, row(g_mem_kv[0]), w_km[0].astype(BF16), w_vm[0].astype(BF16))
    y_p, conv_p, ckv_p, kr_p = _layer(
        x_prompt, jnp.zeros((bp, CONV_WIDTH - 1, d), F32), jnp.zeros((bp, 0, KV_LORA), F32),
        jnp.zeros((bp, 0, QK_ROPE), F32), mk, mv, weights, gains, w_conv[0], g_fin,
        nb=1, tm=256, tl=512, tq=512, tk=512, tm_mlp=512)
    past = cache_ckv.shape[2]
    y_s, conv_s, ckv_s, kr_s = _layer(
        x_sample, cache_conv[0], cache_ckv[0], cache_krope[0],
        cache_mem_k[0].reshape(bs, n_mem, d), cache_mem_v[0].reshape(bs, n_mem, d),
        weights, gains, w_conv[0], g_fin,
        nb=bs, tm=ts, tl=past + ts, tq=ts, tk=past, tm_mlp=bs * ts)
    mshape = (1, bp, n_mem, MEM_HEADS, d // MEM_HEADS)
    return (y_p, y_s, conv_p[None], ckv_p[None], kr_p[None], mk.reshape(mshape), mv.reshape(mshape),
            conv_s[None], ckv_s[None], kr_s[None])
```

```python
import functools

import jax
import jax.numpy as jnp
from jax import lax
from jax.experimental import pallas as pl
from jax.experimental.pallas import tpu as pltpu

CHUNK = 64
CONV_WIDTH = 3
N_HEADS = 16
QK_NOPE = 64
QK_ROPE = 32
V_HEAD = 64
Q_LORA = 512
KV_LORA = 256
ROPE_THETA = 10000.0
MEM_HEADS = 4
EPS = 1e-6
ATTN_SCALE = (QK_NOPE + QK_ROPE) ** -0.5
LOG2_E = 1.4426950408889634

LANES = 128
SUBLANES = 8
HEAD_PAD = LANES
HALF_ROPE = QK_ROPE // 2
VMEM_LIMIT = 56 * 1024 * 1024

BF16 = jnp.bfloat16
F32 = jnp.float32


def _dot(a, b):
    return jnp.dot(a, b, preferred_element_type=F32)


def _dot_nt(a, b):
    return lax.dot_general(a, b, (((1,), (1,)), ((), ())), preferred_element_type=F32)


def _rms(x, g):
    return x * lax.rsqrt(jnp.mean(x * x, axis=-1, keepdims=True) + EPS) * g


def _rope(x, cos_t, sin_a, sin_b):
    return (x * cos_t + pltpu.roll(x, HALF_ROPE, 1) * sin_a
            + pltpu.roll(x, LANES - HALF_ROPE, 1) * sin_b)


def _full(shape):
    return pl.BlockSpec(shape, lambda *_: (0,) * len(shape))


def _params(sem):
    return pltpu.CompilerParams(dimension_semantics=sem, vmem_limit_bytes=VMEM_LIMIT)


def _in_proj_kernel(x_ref, cache_ref, gmix_ref, win_ref, wconv_ref, wco_ref, gq_ref, wuq_ref,
                    gkv_ref, qtab_ref, ktab_ref,
                    ga_ref, sgm_ref, q_ref, ckv_ref, kr_ref, krpad_ref, newconv_ref,
                    vbuf_ref, *, nb, tm, d):
    j = pl.program_id(1)
    m = nb * tm
    c_u, c_b, c_c = 0, d, 2 * d
    c_q = 3 * d
    c_kv = c_q + Q_LORA
    c_kr = c_kv + KV_LORA
    c_ac = c_kr + HEAD_PAD
    c_am = c_ac + d

    x = x_ref[...].reshape(m, d)
    n = _rms(x, gmix_ref[...]).astype(BF16)

    u = _dot(n, win_ref[:, c_u:c_u + d])
    gc = _dot(n, win_ref[:, c_c:c_c + d])
    v = gc * u
    w0 = wconv_ref[0:1, :]
    w1 = wconv_ref[1:2, :]
    w2 = wconv_ref[2:3, :]
    ys = []
    for i in range(nb):
        vi = v[i * tm:(i + 1) * tm]

        @pl.when(j == 0)
        def _():
            vbuf_ref[i, SUBLANES - 2:SUBLANES, :] = cache_ref[i]

        vbuf_ref[i, SUBLANES:SUBLANES + tm, :] = vi
        ys.append(vbuf_ref[i, SUBLANES - 2:SUBLANES - 2 + tm, :] * w0
                  + vbuf_ref[i, SUBLANES - 1:SUBLANES - 1 + tm, :] * w1
                  + vi * w2)
        tail = vbuf_ref[i, SUBLANES + tm - 2:SUBLANES + tm, :]
        vbuf_ref[i, SUBLANES - 2:SUBLANES, :] = tail
        newconv_ref[i] = tail
    y = ys[0] if nb == 1 else jnp.concatenate(ys, axis=0)
    gb = _dot(n, win_ref[:, c_b:c_b + d])
    ya = _dot((gb * y).astype(BF16), wco_ref[...])
    ac = _dot(n, win_ref[:, c_ac:c_ac + d])
    ga_ref[...] = (jax.nn.sigmoid(ac) * ya).reshape(nb, tm, d)
    am = _dot(n, win_ref[:, c_am:c_am + d])
    sgm_ref[...] = jax.nn.sigmoid(am).reshape(nb, tm, d)

    cq = _dot(n, win_ref[:, c_q:c_q + Q_LORA])
    cqn = _rms(cq, gq_ref[...]).astype(BF16)
    qc, qa, qb = qtab_ref[0], qtab_ref[1], qtab_ref[2]
    for h in range(N_HEADS):
        qh = _dot(cqn, wuq_ref[:, h * HEAD_PAD:(h + 1) * HEAD_PAD])
        qh = _rope(qh, qc, qa, qb).astype(BF16)
        for i in range(nb):
            q_ref[i, h] = qh[i * tm:(i + 1) * tm]

    ckv = _dot(n, win_ref[:, c_kv:c_kv + KV_LORA])
    ckv_ref[...] = _rms(ckv, gkv_ref[...]).reshape(nb, tm, KV_LORA)
    kr = _dot(n, win_ref[:, c_kr:c_kr + HEAD_PAD])
    kr = _rope(kr, ktab_ref[0], ktab_ref[1], ktab_ref[2])
    kr_ref[...] = kr[:, :QK_ROPE].reshape(nb, tm, QK_ROPE)
    krpad_ref[...] = kr.astype(BF16).reshape(nb, tm, HEAD_PAD)


def _in_proj(x, conv_cache, g_mix, w_in_p, w_conv, w_co, g_q, w_uq_p, g_kv, qtab, ktab, *, nb, tm):
    b, t, d = x.shape
    m = nb * tm
    ncol = w_in_p.shape[1]
    grid = (b // nb, t // tm)
    row = lambda bi, j: (bi, j, 0)
    out_shape = (
        jax.ShapeDtypeStruct((b, t, d), F32),
        jax.ShapeDtypeStruct((b, t, d), F32),
        jax.ShapeDtypeStruct((b, N_HEADS, t, HEAD_PAD), BF16),
        jax.ShapeDtypeStruct((b, t, KV_LORA), F32),
        jax.ShapeDtypeStruct((b, t, QK_ROPE), F32),
        jax.ShapeDtypeStruct((b, t, HEAD_PAD), BF16),
        jax.ShapeDtypeStruct((b, CONV_WIDTH - 1, d), F32),
    )
    in_specs = [
        pl.BlockSpec((nb, tm, d), row),
        pl.BlockSpec((nb, CONV_WIDTH - 1, d), lambda bi, j: (bi, 0, 0)),
        _full((1, d)),
        _full((d, ncol)),
        _full((CONV_WIDTH, d)),
        _full((d, d)),
        _full((1, Q_LORA)),
        _full((Q_LORA, N_HEADS * HEAD_PAD)),
        _full((1, KV_LORA)),
        pl.BlockSpec((3, m, LANES), lambda bi, j: (0, j, 0)),
        pl.BlockSpec((3, m, LANES), lambda bi, j: (0, j, 0)),
    ]
    out_specs = (
        pl.BlockSpec((nb, tm, d), row),
        pl.BlockSpec((nb, tm, d), row),
        pl.BlockSpec((nb, N_HEADS, tm, HEAD_PAD), lambda bi, j: (bi, 0, j, 0)),
        pl.BlockSpec((nb, tm, KV_LORA), row),
        pl.BlockSpec((nb, tm, QK_ROPE), row),
        pl.BlockSpec((nb, tm, HEAD_PAD), row),
        pl.BlockSpec((nb, CONV_WIDTH - 1, d), lambda bi, j: (bi, 0, 0)),
    )
    return pl.pallas_call(
        functools.partial(_in_proj_kernel, nb=nb, tm=tm, d=d),
        grid=grid, in_specs=in_specs, out_specs=out_specs, out_shape=out_shape,
        scratch_shapes=[pltpu.VMEM((nb, tm + SUBLANES, d), F32)],
        compiler_params=_params(("arbitrary", "arbitrary")),
        name="in_proj",
    )(x, conv_cache, g_mix, w_in_p, w_conv, w_co, g_q, w_uq_p, g_kv, qtab, ktab)


def _kv_expand_kernel(ckv_ref, krpad_ref, wuk_ref, wuv_ref, k_ref, v_ref):
    c = ckv_ref[0].astype(BF16)
    kr = krpad_ref[0].astype(F32)
    ones = (lax.broadcasted_iota(jnp.int32, kr.shape, 1) >= V_HEAD).astype(F32)
    for h in range(N_HEADS):
        cols = slice(h * HEAD_PAD, (h + 1) * HEAD_PAD)
        k_ref[0, h] = (_dot(c, wuk_ref[:, cols]) + kr).astype(BF16)
        v_ref[0, h] = (_dot(c, wuv_ref[:, cols]) + ones).astype(BF16)


def _kv_expand(ckv, kr_pad, w_uk_p, w_uv_p, *, tl):
    b, l, _ = ckv.shape
    head_tiles = pl.BlockSpec((1, N_HEADS, tl, HEAD_PAD), lambda bi, j: (bi, 0, j, 0))
    return pl.pallas_call(
        _kv_expand_kernel,
        grid=(b, l // tl),
        in_specs=[
            pl.BlockSpec((1, tl, KV_LORA), lambda bi, j: (bi, j, 0)),
            pl.BlockSpec((1, tl, HEAD_PAD), lambda bi, j: (bi, j, 0)),
            _full((KV_LORA, N_HEADS * HEAD_PAD)),
            _full((KV_LORA, N_HEADS * HEAD_PAD)),
        ],
        out_specs=(head_tiles, head_tiles),
        out_shape=(jax.ShapeDtypeStruct((b, N_HEADS, l, HEAD_PAD), BF16),) * 2,
        compiler_params=_params(("arbitrary", "arbitrary")),
        name="kv_expand",
    )(ckv, kr_pad, w_uk_p, w_uv_p)


def _attention_kernel(q_ref, k_ref, v_ref, o_ref, m_ref, acc_ref, *, tq, tk, past):
    i = pl.program_id(2)
    m_ref[...] = jnp.full(m_ref.shape, -jnp.inf, F32)
    acc_ref[...] = jnp.zeros(acc_ref.shape, F32)

    def update(start, width, mask):
        for e in range(2):
            s = _dot_nt(q_ref[0, e], k_ref[0, e, pl.ds(start, width), :])
            if mask is not None:
                s = jnp.where(mask, s, -jnp.inf)
            m_old = m_ref[e]
            m_new = jnp.maximum(m_old, jnp.max(s, axis=-1, keepdims=True))
            m_wide = m_new[:, :width] if width < LANES else jnp.tile(m_new, (1, width // LANES))
            p = jnp.exp2(s - m_wide).astype(BF16)
            acc_ref[e] = (jnp.exp2(m_old - m_new) * acc_ref[e]
                          + _dot(p, v_ref[0, e, pl.ds(start, width), :]))
            m_ref[e] = m_new

    def full_blocks(n_blocks):
        def body(kb, carry):
            for u in range(n_blocks):
                update(pl.multiple_of((kb * n_blocks + u) * tk, tk), tk, None)
            return carry
        return body

    n_full = (past + i * tq) // tk
    lax.fori_loop(0, n_full // 2, full_blocks(2), 0)
    lax.fori_loop(n_full - n_full % 2, n_full, full_blocks(1), 0)
    rq = lax.broadcasted_iota(jnp.int32, (tq, tq), 0) // CHUNK
    ck = lax.broadcasted_iota(jnp.int32, (tq, tq), 1) // CHUNK
    update(pl.multiple_of(past + i * tq, tq), tq, ck <= rq)
    acc0, acc1 = acc_ref[0], acc_ref[1]
    first = lax.broadcasted_iota(jnp.int32, (tq, HEAD_PAD), 1) < V_HEAD
    num = jnp.where(first, acc0, pltpu.roll(acc1, V_HEAD, 1))
    den = jnp.where(first, pltpu.roll(acc0, V_HEAD, 1), acc1)
    o_ref[0] = (num / den).astype(o_ref.dtype)


def _attention(q, k, v, *, tq, tk):
    b, _, t, _ = q.shape
    l = k.shape[2]
    past = l - t
    assert tq % CHUNK == 0 and all((past + i * tq) % tk == 0 for i in range(t // tq))
    return pl.pallas_call(
        functools.partial(_attention_kernel, tq=tq, tk=tk, past=past),
        grid=(b, N_HEADS // 2, t // tq),
        in_specs=[
            pl.BlockSpec((1, 2, tq, HEAD_PAD), lambda bi, hp, i: (bi, hp, i, 0)),
            pl.BlockSpec((1, 2, l, HEAD_PAD), lambda bi, hp, i: (bi, hp, 0, 0)),
            pl.BlockSpec((1, 2, l, HEAD_PAD), lambda bi, hp, i: (bi, hp, 0, 0)),
        ],
        out_specs=pl.BlockSpec((1, tq, 2 * V_HEAD), lambda bi, hp, i: (bi, i, hp)),
        out_shape=jax.ShapeDtypeStruct((b, t, N_HEADS * V_HEAD), BF16),
        scratch_shapes=[pltpu.VMEM((2, tq, LANES), F32), pltpu.VMEM((2, tq, HEAD_PAD), F32)],
        compiler_params=_params(("arbitrary", "arbitrary", "arbitrary")),
        name="attention",
    )(q, k, v)


def _post_attn_kernel(x_ref, o_ref, ga_ref, sgm_ref, wmo_ref, wmix_ref, gmq_ref, wqm_ref,
                      mk_ref, mv_ref, wom_ref, x2_ref, om_ref, *, nb, tm, d):
    m = nb * tm
    hd = d // MEM_HEADS
    mem_scale = hd ** -0.5
    x = x_ref[...].reshape(m, d)
    yb = _dot(o_ref[...].reshape(m, d), wmo_ref[...])
    mix = ga_ref[...].reshape(m, d) + sgm_ref[...].reshape(m, d) * yb
    x1 = x + _dot(mix.astype(BF16), wmix_ref[...])
    hq = _rms(x1, gmq_ref[...]).astype(BF16)
    qm = (_dot(hq, wqm_ref[...]) * mem_scale).astype(BF16)
    for i in range(nb):
        mk = mk_ref[i].astype(BF16)
        mv = mv_ref[i].astype(BF16)
        for h in range(MEM_HEADS):
            cols = slice(h * hd, (h + 1) * hd)
            s = _dot_nt(qm[i * tm:(i + 1) * tm, cols], mk[:, cols])
            p = jnp.exp(s - jnp.max(s, axis=-1, keepdims=True))
            l = jnp.sum(p, axis=-1, keepdims=True)
            oh = _dot(p.astype(BF16), mv[:, cols]) / l
            om_ref[i * tm:(i + 1) * tm, cols] = oh.astype(BF16)
    x2 = x1 + _dot(om_ref[...], wom_ref[...])
    x2_ref[...] = x2.reshape(nb, tm, d)


def _post_attn(x, o, ga, sgm, w_mo, w_mix, g_mq, w_qm, mem_k, mem_v, w_om, *, nb, tm):
    b, t, d = x.shape
    n_mem = mem_k.shape[1]
    row = lambda bi, j: (bi, j, 0)
    per_b = lambda bi, j: (bi, 0, 0)
    return pl.pallas_call(
        functools.partial(_post_attn_kernel, nb=nb, tm=tm, d=d),
        grid=(b // nb, t // tm),
        in_specs=[
            pl.BlockSpec((nb, tm, d), row),
            pl.BlockSpec((nb, tm, d), row),
            pl.BlockSpec((nb, tm, d), row),
            pl.BlockSpec((nb, tm, d), row),
            _full((d, d)), _full((d, d)), _full((1, d)), _full((d, d)),
            pl.BlockSpec((nb, n_mem, d), per_b),
            pl.BlockSpec((nb, n_mem, d), per_b),
            _full((d, d)),
        ],
        out_specs=pl.BlockSpec((nb, tm, d), row),
        out_shape=jax.ShapeDtypeStruct((b, t, d), F32),
        scratch_shapes=[pltpu.VMEM((nb * tm, d), BF16)],
        compiler_params=_params(("arbitrary", "arbitrary")),
        name="post_attn",
    )(x, o, ga, sgm, w_mo, w_mix, g_mq, w_qm, mem_k, mem_v, w_om)


def _mlp_kernel(x_ref, gmlp_ref, wup_ref, wdown_ref, gfin_ref, y_ref, *, fchunk):
    x = x_ref[...]
    hm = _rms(x, gmlp_ref[...]).astype(BF16)
    acc = x
    for c in range(wup_ref.shape[1] // fchunk):
        cols = slice(c * fchunk, (c + 1) * fchunk)
        hcol = jnp.maximum(_dot(hm, wup_ref[:, cols]), 0.0)
        acc = acc + _dot((hcol * hcol).astype(BF16), wdown_ref[cols, :])
    y_ref[...] = _rms(acc, gfin_ref[...])


def _mlp(x, g_mlp, w_up, w_down, g_final, *, tm):
    n, d = x.shape
    f = w_up.shape[1]
    return pl.pallas_call(
        functools.partial(_mlp_kernel, fchunk=min(f, 1024)),
        grid=(n // tm,),
        in_specs=[
            pl.BlockSpec((tm, d), lambda r: (r, 0)),
            _full((1, d)), _full((d, f)), _full((f, d)), _full((1, d)),
        ],
        out_specs=pl.BlockSpec((tm, d), lambda r: (r, 0)),
        out_shape=jax.ShapeDtypeStruct((n, d), F32),
        compiler_params=_params(("arbitrary",)),
        name="mlp",
    )(x, g_mlp, w_up, w_down, g_final)


def _mem_kv_kernel(mem_ref, g_ref, wk_ref, wv_ref, k_ref, v_ref):
    mn = _rms(mem_ref[0], g_ref[...]).astype(BF16)
    k_ref[0] = _dot(mn, wk_ref[...])
    v_ref[0] = _dot(mn, wv_ref[...])


def _mem_kv(mem, g, w_k, w_v):
    b, n_mem, d = mem.shape
    blk = pl.BlockSpec((1, n_mem, d), lambda bi: (bi, 0, 0))
    return pl.pallas_call(
        _mem_kv_kernel,
        grid=(b,),
        in_specs=[blk, _full((1, d)), _full((d, d)), _full((d, d))],
        out_specs=(blk, blk),
        out_shape=(jax.ShapeDtypeStruct((b, n_mem, d), F32),) * 2,
        compiler_params=_params(("arbitrary",)),
        name="mem_kv",
    )(mem, g, w_k, w_v)


def _rope_tables(pos, scale):
    t = pos.shape[0]
    inv = ROPE_THETA ** (-jnp.arange(HALF_ROPE, dtype=F32) / HALF_ROPE)
    ang = pos.astype(F32)[:, None] * inv[None, :]
    cos, sin = jnp.cos(ang), jnp.sin(ang)
    z = lambda w: jnp.zeros((t, w), F32)
    cos_t = jnp.concatenate([cos, cos, jnp.ones((t, LANES - QK_ROPE), F32)], axis=1)
    sin_a = jnp.concatenate([z(HALF_ROPE), sin, z(LANES - QK_ROPE)], axis=1)
    sin_b = jnp.concatenate([-sin, z(LANES - HALF_ROPE)], axis=1)
    return jnp.stack([cos_t, sin_a, sin_b]) * scale


def _prep_weights(w_in, w_conv_out, w_uq, w_ukv, w_mla_out, w_mix_out, w_qm, w_om, w_up, w_down):
    d = w_in.shape[0]
    c_kr = 3 * d + Q_LORA + KV_LORA
    w_in_p = jnp.concatenate(
        [w_in[:, :c_kr], w_in[:, c_kr:c_kr + QK_ROPE],
         jnp.zeros((d, HEAD_PAD - QK_ROPE), w_in.dtype), w_in[:, c_kr + QK_ROPE:]], axis=1)
    pad = HEAD_PAD - QK_ROPE - QK_NOPE
    wq = w_uq.reshape(Q_LORA, N_HEADS, QK_NOPE + QK_ROPE)
    w_uq_p = jnp.concatenate(
        [wq[..., QK_NOPE:], wq[..., :QK_NOPE], jnp.zeros((Q_LORA, N_HEADS, pad), w_uq.dtype)],
        axis=-1).reshape(Q_LORA, N_HEADS * HEAD_PAD)
    wkv = w_ukv.reshape(KV_LORA, N_HEADS, QK_NOPE + V_HEAD)
    w_uk_p = jnp.concatenate(
        [jnp.zeros((KV_LORA, N_HEADS, QK_ROPE), w_ukv.dtype), wkv[..., :QK_NOPE],
         jnp.zeros((KV_LORA, N_HEADS, pad), w_ukv.dtype)], axis=-1).reshape(KV_LORA, N_HEADS * HEAD_PAD)
    w_uv_p = jnp.concatenate(
        [wkv[..., QK_NOPE:], jnp.zeros((KV_LORA, N_HEADS, HEAD_PAD - V_HEAD), w_ukv.dtype)],
        axis=-1).reshape(KV_LORA, N_HEADS * HEAD_PAD)
    cast = lambda w: w.astype(BF16)
    return tuple(map(cast, (w_in_p, w_conv_out, w_uq_p, w_uk_p, w_uv_p, w_mla_out, w_mix_out,
                            w_qm, w_om, w_up, w_down)))


def _layer(x, conv_cache, past_ckv, past_kr, mem_k, mem_v, weights, gains, w_conv, g_final,
           *, nb, tm, tl, tq, tk, tm_mlp):
    (w_in_p, w_co, w_uq_p, w_uk_p, w_uv, w_mo, w_mix, w_qm, w_om, w_up, w_down) = weights
    g_mix, g_q, g_kv, g_mem_q, g_mlp = gains
    b, t, d = x.shape
    past = past_ckv.shape[1]
    pos = past + jnp.arange(t, dtype=jnp.int32)
    qtab = jnp.tile(_rope_tables(pos, ATTN_SCALE * LOG2_E), (1, nb, 1))
    ktab = jnp.tile(_rope_tables(pos, 1.0), (1, nb, 1))
    ga, sgm, q, ckv, kr, kr_pad, new_conv = _in_proj(
        x, conv_cache, g_mix, w_in_p, w_conv, w_co, g_q, w_uq_p, g_kv, qtab, ktab, nb=nb, tm=tm)
    if past:
        past_kr_pad = jnp.pad(past_kr, ((0, 0), (0, 0), (0, HEAD_PAD - QK_ROPE))).astype(BF16)
        ckv_all = jnp.concatenate([past_ckv, ckv], axis=1)
        kr_all = jnp.concatenate([past_kr_pad, kr_pad], axis=1)
    else:
        ckv_all, kr_all = ckv, kr_pad
    k, v = _kv_expand(ckv_all, kr_all, w_uk_p, w_uv, tl=tl)
    o = _attention(q, k, v, tq=tq, tk=tk)
    x2 = _post_attn(x, o, ga, sgm, w_mo, w_mix, g_mem_q, w_qm, mem_k, mem_v, w_om, nb=nb, tm=tm)
    y = _mlp(x2.reshape(b * t, d), g_mlp, w_up, w_down, g_final, tm=tm_mlp).reshape(b, t, d)
    return y, new_conv, ckv, kr


def kernel(x_prompt, x_sample, cache_conv, cache_ckv, cache_krope, cache_mem_k, cache_mem_v, mem_prompt, g_mix, w_in, w_conv, w_conv_out, g_q, w_uq, g_kv, w_ukv, w_mla_out, w_mix_out, g_mem_q, g_mem_kv, w_qm, w_km, w_vm, w_om, g_mlp, w_up, w_down, g_final):
    depth = w_in.shape[0]
    assert depth == 1, "single-layer step"
    bp, tp, d = x_prompt.shape
    bs, ts, _ = x_sample.shape
    n_mem = mem_prompt.shape[1]
    row = lambda g: g.reshape(1, -1)
    weights = _prep_weights(w_in[0], w_conv_out[0], w_uq[0], w_ukv[0], w_mla_out[0], w_mix_out[0],
                            w_qm[0], w_om[0], w_up[0], w_down[0])
    gains = (row(g_mix[0]), row(g_q[0]), row(g_kv[0]), row(g_mem_q[0]), row(g_mlp[0]))
    g_fin = row(g_final)

    mk, mv = _mem_kv(mem_prompt, row(g_mem_kv[0]), w_km[0].astype(BF16), w_vm[0].astype(BF16))
    y_p, conv_p, ckv_p, kr_p = _layer(
        x_prompt, jnp.zeros((bp, CONV_WIDTH - 1, d), F32), jnp.zeros((bp, 0, KV_LORA), F32),
        jnp.zeros((bp, 0, QK_ROPE), F32), mk, mv, weights, gains, w_conv[0], g_fin,
        nb=1, tm=512, tl=512, tq=512, tk=512, tm_mlp=512)
    past = cache_ckv.shape[2]
    y_s, conv_s, ckv_s, kr_s = _layer(
        x_sample, cache_conv[0], cache_ckv[0], cache_krope[0],
        cache_mem_k[0].reshape(bs, n_mem, d), cache_mem_v[0].reshape(bs, n_mem, d),
        weights, gains, w_conv[0], g_fin,
        nb=bs, tm=ts, tl=past + ts, tq=ts, tk=past, tm_mlp=bs * ts)
    mshape = (1, bp, n_mem, MEM_HEADS, d // MEM_HEADS)
    return (y_p, y_s, conv_p[None], ckv_p[None], kr_p[None], mk.reshape(mshape), mv.reshape(mshape),
            conv_s[None], ckv_s[None], kr_s[None])
```

```python
import functools

import jax
import jax.numpy as jnp
from jax import lax
from jax.experimental import pallas as pl
from jax.experimental.pallas import tpu as pltpu

CHUNK = 64
CONV_WIDTH = 3
N_HEADS = 16
QK_NOPE = 64
QK_ROPE = 32
V_HEAD = 64
Q_LORA = 512
KV_LORA = 256
ROPE_THETA = 10000.0
MEM_HEADS = 4
EPS = 1e-6
ATTN_SCALE = (QK_NOPE + QK_ROPE) ** -0.5
LOG2_E = 1.4426950408889634

LANES = 128
SUBLANES = 8
HEAD_PAD = LANES
HALF_ROPE = QK_ROPE // 2
VMEM_LIMIT = 56 * 1024 * 1024

BF16 = jnp.bfloat16
F32 = jnp.float32


def _dot(a, b):
    return jnp.dot(a, b, preferred_element_type=F32)


def _dot_nt(a, b):
    return lax.dot_general(a, b, (((1,), (1,)), ((), ())), preferred_element_type=F32)


def _rms(x, g):
    return x * lax.rsqrt(jnp.mean(x * x, axis=-1, keepdims=True) + EPS) * g


def _rope(x, cos_t, sin_a, sin_b):
    return (x * cos_t + pltpu.roll(x, HALF_ROPE, 1) * sin_a
            + pltpu.roll(x, LANES - HALF_ROPE, 1) * sin_b)


def _full(shape):
    return pl.BlockSpec(shape, lambda *_: (0,) * len(shape))


def _params(sem):
    return pltpu.CompilerParams(dimension_semantics=sem, vmem_limit_bytes=VMEM_LIMIT)


def _expand_kv(c, kr, wuk_ref, wuv_ref, store_k, store_v):
    ones = (lax.broadcasted_iota(jnp.int32, kr.shape, 1) >= V_HEAD).astype(F32)
    for h in range(N_HEADS):
        cols = slice(h * HEAD_PAD, (h + 1) * HEAD_PAD)
        store_k(h, (_dot(c, wuk_ref[:, cols]) + kr).astype(BF16))
        store_v(h, (_dot(c, wuv_ref[:, cols]) + ones).astype(BF16))


def _in_proj_kernel(x_ref, cache_ref, gmix_ref, win_ref, wconv_ref, wco_ref, gq_ref, wuq_ref,
                    gkv_ref, qtab_ref, ktab_ref, *rest, nb, tm, d, fuse_kv):
    if fuse_kv:
        (wuk_ref, wuv_ref, ga_ref, sgm_ref, q_ref, ckv_ref, kr_ref, k_ref, v_ref, newconv_ref,
         vbuf_ref) = rest
    else:
        ga_ref, sgm_ref, q_ref, ckv_ref, kr_ref, krpad_ref, newconv_ref, vbuf_ref = rest
    j = pl.program_id(1)
    m = nb * tm
    c_u, c_b, c_c = 0, d, 2 * d
    c_q = 3 * d
    c_kv = c_q + Q_LORA
    c_kr = c_kv + KV_LORA
    c_ac = c_kr + HEAD_PAD
    c_am = c_ac + d

    x = x_ref[...].reshape(m, d)
    n = _rms(x, gmix_ref[...]).astype(BF16)

    u = _dot(n, win_ref[:, c_u:c_u + d])
    gc = _dot(n, win_ref[:, c_c:c_c + d])
    v = gc * u
    w0 = wconv_ref[0:1, :]
    w1 = wconv_ref[1:2, :]
    w2 = wconv_ref[2:3, :]
    ys = []
    for i in range(nb):
        vi = v[i * tm:(i + 1) * tm]

        @pl.when(j == 0)
        def _():
            vbuf_ref[i, SUBLANES - 2:SUBLANES, :] = cache_ref[i]

        vbuf_ref[i, SUBLANES:SUBLANES + tm, :] = vi
        ys.append(vbuf_ref[i, SUBLANES - 2:SUBLANES - 2 + tm, :] * w0
                  + vbuf_ref[i, SUBLANES - 1:SUBLANES - 1 + tm, :] * w1
                  + vi * w2)
        tail = vbuf_ref[i, SUBLANES + tm - 2:SUBLANES + tm, :]
        vbuf_ref[i, SUBLANES - 2:SUBLANES, :] = tail
        newconv_ref[i] = tail
    y = ys[0] if nb == 1 else jnp.concatenate(ys, axis=0)
    gb = _dot(n, win_ref[:, c_b:c_b + d])
    ya = _dot((gb * y).astype(BF16), wco_ref[...])
    ac = _dot(n, win_ref[:, c_ac:c_ac + d])
    ga_ref[...] = (jax.nn.sigmoid(ac) * ya).reshape(nb, tm, d)
    am = _dot(n, win_ref[:, c_am:c_am + d])
    sgm_ref[...] = jax.nn.sigmoid(am).reshape(nb, tm, d)

    cq = _dot(n, win_ref[:, c_q:c_q + Q_LORA])
    cqn = _rms(cq, gq_ref[...]).astype(BF16)
    qc, qa, qb = qtab_ref[0], qtab_ref[1], qtab_ref[2]
    for h in range(N_HEADS):
        qh = _dot(cqn, wuq_ref[:, h * HEAD_PAD:(h + 1) * HEAD_PAD])
        qh = _rope(qh, qc, qa, qb).astype(BF16)
        for i in range(nb):
            q_ref[i, h] = qh[i * tm:(i + 1) * tm]

    ckv = _rms(_dot(n, win_ref[:, c_kv:c_kv + KV_LORA]), gkv_ref[...])
    ckv_ref[...] = ckv.reshape(nb, tm, KV_LORA)
    kr = _dot(n, win_ref[:, c_kr:c_kr + HEAD_PAD])
    kr = _rope(kr, ktab_ref[0], ktab_ref[1], ktab_ref[2])
    kr_ref[...] = kr[:, :QK_ROPE].reshape(nb, tm, QK_ROPE)
    if fuse_kv:
        def store_k(h, kh):
            k_ref[:, h] = kh.reshape(nb, tm, HEAD_PAD)

        def store_v(h, vh):
            v_ref[:, h] = vh.reshape(nb, tm, HEAD_PAD)

        _expand_kv(ckv.astype(BF16), kr, wuk_ref, wuv_ref, store_k, store_v)
    else:
        krpad_ref[...] = kr.astype(BF16).reshape(nb, tm, HEAD_PAD)


def _in_proj(x, conv_cache, g_mix, w_in_p, w_conv, w_co, g_q, w_uq_p, g_kv, qtab, ktab,
             w_uk_p=None, w_uv_p=None, *, nb, tm):
    fuse_kv = w_uk_p is not None
    b, t, d = x.shape
    m = nb * tm
    ncol = w_in_p.shape[1]
    grid = (b // nb, t // tm)
    row = lambda bi, j: (bi, j, 0)
    head_shape = jax.ShapeDtypeStruct((b, N_HEADS, t, HEAD_PAD), BF16)
    head_spec = pl.BlockSpec((nb, N_HEADS, tm, HEAD_PAD), lambda bi, j: (bi, 0, j, 0))
    if fuse_kv:
        kv_shapes, kv_specs = (head_shape, head_shape), (head_spec, head_spec)
        kv_weights = (w_uk_p, w_uv_p)
    else:
        kv_shapes = (jax.ShapeDtypeStruct((b, t, HEAD_PAD), BF16),)
        kv_specs = (pl.BlockSpec((nb, tm, HEAD_PAD), row),)
        kv_weights = ()
    out_shape = (
        jax.ShapeDtypeStruct((b, t, d), F32),
        jax.ShapeDtypeStruct((b, t, d), F32),
        head_shape,
        jax.ShapeDtypeStruct((b, t, KV_LORA), F32),
        jax.ShapeDtypeStruct((b, t, QK_ROPE), F32),
        *kv_shapes,
        jax.ShapeDtypeStruct((b, CONV_WIDTH - 1, d), F32),
    )
    in_specs = [
        pl.BlockSpec((nb, tm, d), row),
        pl.BlockSpec((nb, CONV_WIDTH - 1, d), lambda bi, j: (bi, 0, 0)),
        _full((1, d)),
        _full((d, ncol)),
        _full((CONV_WIDTH, d)),
        _full((d, d)),
        _full((1, Q_LORA)),
        _full((Q_LORA, N_HEADS * HEAD_PAD)),
        _full((1, KV_LORA)),
        pl.BlockSpec((3, m, LANES), lambda bi, j: (0, j, 0)),
        pl.BlockSpec((3, m, LANES), lambda bi, j: (0, j, 0)),
        *[_full(w.shape) for w in kv_weights],
    ]
    out_specs = (
        pl.BlockSpec((nb, tm, d), row),
        pl.BlockSpec((nb, tm, d), row),
        head_spec,
        pl.BlockSpec((nb, tm, KV_LORA), row),
        pl.BlockSpec((nb, tm, QK_ROPE), row),
        *kv_specs,
        pl.BlockSpec((nb, CONV_WIDTH - 1, d), lambda bi, j: (bi, 0, 0)),
    )
    return pl.pallas_call(
        functools.partial(_in_proj_kernel, nb=nb, tm=tm, d=d, fuse_kv=fuse_kv),
        grid=grid, in_specs=in_specs, out_specs=out_specs, out_shape=out_shape,
        scratch_shapes=[pltpu.VMEM((nb, tm + SUBLANES, d), F32)],
        compiler_params=_params(("arbitrary", "arbitrary")),
        name="in_proj",
    )(x, conv_cache, g_mix, w_in_p, w_conv, w_co, g_q, w_uq_p, g_kv, qtab, ktab, *kv_weights)


def _kv_expand_kernel(ckv_ref, krpad_ref, wuk_ref, wuv_ref, k_ref, v_ref):
    def store_k(h, kh):
        k_ref[0, h] = kh

    def store_v(h, vh):
        v_ref[0, h] = vh

    _expand_kv(ckv_ref[0].astype(BF16), krpad_ref[0].astype(F32), wuk_ref, wuv_ref, store_k, store_v)


def _kv_expand(ckv, kr_pad, w_uk_p, w_uv_p, *, tl):
    b, l, _ = ckv.shape
    head_tiles = pl.BlockSpec((1, N_HEADS, tl, HEAD_PAD), lambda bi, j: (bi, 0, j, 0))
    return pl.pallas_call(
        _kv_expand_kernel,
        grid=(b, l // tl),
        in_specs=[
            pl.BlockSpec((1, tl, KV_LORA), lambda bi, j: (bi, j, 0)),
            pl.BlockSpec((1, tl, HEAD_PAD), lambda bi, j: (bi, j, 0)),
            _full((KV_LORA, N_HEADS * HEAD_PAD)),
            _full((KV_LORA, N_HEADS * HEAD_PAD)),
        ],
        out_specs=(head_tiles, head_tiles),
        out_shape=(jax.ShapeDtypeStruct((b, N_HEADS, l, HEAD_PAD), BF16),) * 2,
        compiler_params=_params(("arbitrary", "arbitrary")),
        name="kv_expand",
    )(ckv, kr_pad, w_uk_p, w_uv_p)


def _attention_kernel(q_ref, k_ref, v_ref, o_ref, m_ref, acc_ref, *, nh, tq, tk, past):
    i = pl.program_id(2)
    m_ref[...] = jnp.full(m_ref.shape, -jnp.inf, F32)
    acc_ref[...] = jnp.zeros(acc_ref.shape, F32)

    def update(start, width, mask=None):
        for e in range(nh):
            s = _dot_nt(q_ref[0, e], k_ref[0, e, pl.ds(start, width), :])
            if mask is not None:
                s = jnp.where(mask, s, -jnp.inf)
            m_old = m_ref[e]
            m_new = jnp.maximum(m_old, jnp.max(s, axis=-1, keepdims=True))
            m_wide = m_new[:, :width] if width < LANES else jnp.tile(m_new, (1, width // LANES))
            p = jnp.exp2(s - m_wide).astype(BF16)
            acc_ref[e] = (jnp.exp2(m_old - m_new) * acc_ref[e]
                          + _dot(p, v_ref[0, e, pl.ds(start, width), :]))
            m_ref[e] = m_new

    def full_blocks(n_blocks):
        def body(kb, carry):
            for u in range(n_blocks):
                update(pl.multiple_of((kb * n_blocks + u) * tk, tk), tk)
            return carry
        return body

    n_full = (past + i * tq) // tk
    lax.fori_loop(0, n_full // 2, full_blocks(2), 0)
    lax.fori_loop(n_full - n_full % 2, n_full, full_blocks(1), 0)
    rq = lax.broadcasted_iota(jnp.int32, (tq, tq), 0) // CHUNK
    ck = lax.broadcasted_iota(jnp.int32, (tq, tq), 1) // CHUNK
    update(pl.multiple_of(past + i * tq, tq), tq, ck <= rq)
    first = lax.broadcasted_iota(jnp.int32, (tq, HEAD_PAD), 1) < V_HEAD
    for pair in range(nh // 2):
        acc0, acc1 = acc_ref[2 * pair], acc_ref[2 * pair + 1]
        num = jnp.where(first, acc0, pltpu.roll(acc1, V_HEAD, 1))
        den = jnp.where(first, pltpu.roll(acc0, V_HEAD, 1), acc1)
        o_ref[0, :, pair * HEAD_PAD:(pair + 1) * HEAD_PAD] = (num / den).astype(o_ref.dtype)


def _attention(q, k, v, *, nh, tq, tk):
    b, _, t, _ = q.shape
    l = k.shape[2]
    past = l - t
    assert nh % 2 == 0 and N_HEADS % nh == 0
    assert tq % CHUNK == 0 and all((past + i * tq) % tk == 0 for i in range(t // tq))
    kv_spec = pl.BlockSpec((1, nh, l, HEAD_PAD), lambda bi, hp, i: (bi, hp, 0, 0))
    return pl.pallas_call(
        functools.partial(_attention_kernel, nh=nh, tq=tq, tk=tk, past=past),
        grid=(b, N_HEADS // nh, t // tq),
        in_specs=[
            pl.BlockSpec((1, nh, tq, HEAD_PAD), lambda bi, hp, i: (bi, hp, i, 0)),
            kv_spec, kv_spec,
        ],
        out_specs=pl.BlockSpec((1, tq, nh * V_HEAD), lambda bi, hp, i: (bi, i, hp)),
        out_shape=jax.ShapeDtypeStruct((b, t, N_HEADS * V_HEAD), BF16),
        scratch_shapes=[pltpu.VMEM((nh, tq, LANES), F32), pltpu.VMEM((nh, tq, HEAD_PAD), F32)],
        compiler_params=_params(("arbitrary", "arbitrary", "arbitrary")),
        name="attention",
    )(q, k, v)


def _post_attn_kernel(x_ref, o_ref, ga_ref, sgm_ref, wmo_ref, wmix_ref, gmq_ref, wqm_ref,
                      mk_ref, mv_ref, wom_ref, x2_ref, om_ref, *, nb, tm, d):
    m = nb * tm
    hd = d // MEM_HEADS
    mem_scale = hd ** -0.5
    x = x_ref[...].reshape(m, d)
    yb = _dot(o_ref[...].reshape(m, d), wmo_ref[...])
    mix = ga_ref[...].reshape(m, d) + sgm_ref[...].reshape(m, d) * yb
    x1 = x + _dot(mix.astype(BF16), wmix_ref[...])
    hq = _rms(x1, gmq_ref[...]).astype(BF16)
    qm = (_dot(hq, wqm_ref[...]) * mem_scale).astype(BF16)
    for i in range(nb):
        mk = mk_ref[i].astype(BF16)
        mv = mv_ref[i].astype(BF16)
        for h in range(MEM_HEADS):
            cols = slice(h * hd, (h + 1) * hd)
            s = _dot_nt(qm[i * tm:(i + 1) * tm, cols], mk[:, cols])
            p = jnp.exp(s - jnp.max(s, axis=-1, keepdims=True))
            l = jnp.sum(p, axis=-1, keepdims=True)
            oh = _dot(p.astype(BF16), mv[:, cols]) / l
            om_ref[i * tm:(i + 1) * tm, cols] = oh.astype(BF16)
    x2 = x1 + _dot(om_ref[...], wom_ref[...])
    x2_ref[...] = x2.reshape(nb, tm, d)


def _post_attn(x, o, ga, sgm, w_mo, w_mix, g_mq, w_qm, mem_k, mem_v, w_om, *, nb, tm):
    b, t, d = x.shape
    n_mem = mem_k.shape[1]
    row = lambda bi, j: (bi, j, 0)
    per_b = lambda bi, j: (bi, 0, 0)
    return pl.pallas_call(
        functools.partial(_post_attn_kernel, nb=nb, tm=tm, d=d),
        grid=(b // nb, t // tm),
        in_specs=[
            pl.BlockSpec((nb, tm, d), row),
            pl.BlockSpec((nb, tm, d), row),
            pl.BlockSpec((nb, tm, d), row),
            pl.BlockSpec((nb, tm, d), row),
            _full((d, d)), _full((d, d)), _full((1, d)), _full((d, d)),
            pl.BlockSpec((nb, n_mem, d), per_b),
            pl.BlockSpec((nb, n_mem, d), per_b),
            _full((d, d)),
        ],
        out_specs=pl.BlockSpec((nb, tm, d), row),
        out_shape=jax.ShapeDtypeStruct((b, t, d), F32),
        scratch_shapes=[pltpu.VMEM((nb * tm, d), BF16)],
        compiler_params=_params(("arbitrary", "arbitrary")),
        name="post_attn",
    )(x, o, ga, sgm, w_mo, w_mix, g_mq, w_qm, mem_k, mem_v, w_om)


def _mlp_kernel(x_ref, gmlp_ref, wup_ref, wdown_ref, gfin_ref, y_ref, *, fchunk):
    x = x_ref[...]
    hm = _rms(x, gmlp_ref[...]).astype(BF16)
    acc = x
    for c in range(wup_ref.shape[1] // fchunk):
        cols = slice(c * fchunk, (c + 1) * fchunk)
        hcol = jnp.maximum(_dot(hm, wup_ref[:, cols]), 0.0)
        acc = acc + _dot((hcol * hcol).astype(BF16), wdown_ref[cols, :])
    y_ref[...] = _rms(acc, gfin_ref[...])


def _mlp(x, g_mlp, w_up, w_down, g_final, *, tm):
    n, d = x.shape
    f = w_up.shape[1]
    return pl.pallas_call(
        functools.partial(_mlp_kernel, fchunk=min(f, 1024)),
        grid=(n // tm,),
        in_specs=[
            pl.BlockSpec((tm, d), lambda r: (r, 0)),
            _full((1, d)), _full((d, f)), _full((f, d)), _full((1, d)),
        ],
        out_specs=pl.BlockSpec((tm, d), lambda r: (r, 0)),
        out_shape=jax.ShapeDtypeStruct((n, d), F32),
        compiler_params=_params(("arbitrary",)),
        name="mlp",
    )(x, g_mlp, w_up, w_down, g_final)


def _mem_kv_kernel(mem_ref, g_ref, wk_ref, wv_ref, k_ref, v_ref):
    mn = _rms(mem_ref[0], g_ref[...]).astype(BF16)
    k_ref[0] = _dot(mn, wk_ref[...])
    v_ref[0] = _dot(mn, wv_ref[...])


def _mem_kv(mem, g, w_k, w_v):
    b, n_mem, d = mem.shape
    blk = pl.BlockSpec((1, n_mem, d), lambda bi: (bi, 0, 0))
    return pl.pallas_call(
        _mem_kv_kernel,
        grid=(b,),
        in_specs=[blk, _full((1, d)), _full((d, d)), _full((d, d))],
        out_specs=(blk, blk),
        out_shape=(jax.ShapeDtypeStruct((b, n_mem, d), F32),) * 2,
        compiler_params=_params(("arbitrary",)),
        name="mem_kv",
    )(mem, g, w_k, w_v)


def _rope_tables(pos, scale):
    t = pos.shape[0]
    inv = ROPE_THETA ** (-jnp.arange(HALF_ROPE, dtype=F32) / HALF_ROPE)
    ang = pos.astype(F32)[:, None] * inv[None, :]
    cos, sin = jnp.cos(ang), jnp.sin(ang)
    z = lambda w: jnp.zeros((t, w), F32)
    cos_t = jnp.concatenate([cos, cos, jnp.ones((t, LANES - QK_ROPE), F32)], axis=1)
    sin_a = jnp.concatenate([z(HALF_ROPE), sin, z(LANES - QK_ROPE)], axis=1)
    sin_b = jnp.concatenate([-sin, z(LANES - HALF_ROPE)], axis=1)
    return jnp.stack([cos_t, sin_a, sin_b]) * scale


def _prep_weights(w_in, w_conv_out, w_uq, w_ukv, w_mla_out, w_mix_out, w_qm, w_om, w_up, w_down):
    d = w_in.shape[0]
    c_kr = 3 * d + Q_LORA + KV_LORA
    w_in_p = jnp.concatenate(
        [w_in[:, :c_kr], w_in[:, c_kr:c_kr + QK_ROPE],
         jnp.zeros((d, HEAD_PAD - QK_ROPE), w_in.dtype), w_in[:, c_kr + QK_ROPE:]], axis=1)
    pad = HEAD_PAD - QK_ROPE - QK_NOPE
    wq = w_uq.reshape(Q_LORA, N_HEADS, QK_NOPE + QK_ROPE)
    w_uq_p = jnp.concatenate(
        [wq[..., QK_NOPE:], wq[..., :QK_NOPE], jnp.zeros((Q_LORA, N_HEADS, pad), w_uq.dtype)],
        axis=-1).reshape(Q_LORA, N_HEADS * HEAD_PAD)
    wkv = w_ukv.reshape(KV_LORA, N_HEADS, QK_NOPE + V_HEAD)
    w_uk_p = jnp.concatenate(
        [jnp.zeros((KV_LORA, N_HEADS, QK_ROPE), w_ukv.dtype), wkv[..., :QK_NOPE],
         jnp.zeros((KV_LORA, N_HEADS, pad), w_ukv.dtype)], axis=-1).reshape(KV_LORA, N_HEADS * HEAD_PAD)
    w_uv_p = jnp.concatenate(
        [wkv[..., QK_NOPE:], jnp.zeros((KV_LORA, N_HEADS, HEAD_PAD - V_HEAD), w_ukv.dtype)],
        axis=-1).reshape(KV_LORA, N_HEADS * HEAD_PAD)
    cast = lambda w: w.astype(BF16)
    return tuple(map(cast, (w_in_p, w_conv_out, w_uq_p, w_uk_p, w_uv_p, w_mla_out, w_mix_out,
                            w_qm, w_om, w_up, w_down)))


def _layer(x, conv_cache, past_ckv, past_kr, mem_k, mem_v, weights, gains, w_conv, g_final,
           *, nb, tm, nh, tq, tk, tm_mlp):
    (w_in_p, w_co, w_uq_p, w_uk_p, w_uv_p, w_mo, w_mix, w_qm, w_om, w_up, w_down) = weights
    g_mix, g_q, g_kv, g_mem_q, g_mlp = gains
    b, t, d = x.shape
    past = past_ckv.shape[1]
    pos = past + jnp.arange(t, dtype=jnp.int32)
    qtab = jnp.tile(_rope_tables(pos, ATTN_SCALE * LOG2_E), (1, nb, 1))
    ktab = jnp.tile(_rope_tables(pos, 1.0), (1, nb, 1))
    proj_args = (x, conv_cache, g_mix, w_in_p, w_conv, w_co, g_q, w_uq_p, g_kv, qtab, ktab)
    if past:
        ga, sgm, q, ckv, kr, kr_pad, new_conv = _in_proj(*proj_args, nb=nb, tm=tm)
        past_kr_pad = jnp.pad(past_kr, ((0, 0), (0, 0), (0, HEAD_PAD - QK_ROPE))).astype(BF16)
        k, v = _kv_expand(jnp.concatenate([past_ckv, ckv], axis=1),
                          jnp.concatenate([past_kr_pad, kr_pad], axis=1), w_uk_p, w_uv_p, tl=past + t)
    else:
        ga, sgm, q, ckv, kr, k, v, new_conv = _in_proj(*proj_args, w_uk_p, w_uv_p, nb=nb, tm=tm)
    o = _attention(q, k, v, nh=nh, tq=tq, tk=tk)
    x2 = _post_attn(x, o, ga, sgm, w_mo, w_mix, g_mem_q, w_qm, mem_k, mem_v, w_om, nb=nb, tm=tm)
    y = _mlp(x2.reshape(b * t, d), g_mlp, w_up, w_down, g_final, tm=tm_mlp).reshape(b, t, d)
    return y, new_conv, ckv, kr


def kernel(x_prompt, x_sample, cache_conv, cache_ckv, cache_krope, cache_mem_k, cache_mem_v, mem_prompt, g_mix, w_in, w_conv, w_conv_out, g_q, w_uq, g_kv, w_ukv, w_mla_out, w_mix_out, g_mem_q, g_mem_kv, w_qm, w_km, w_vm, w_om, g_mlp, w_up, w_down, g_final):
    depth = w_in.shape[0]
    assert depth == 1, "single-layer step"
    bp, tp, d = x_prompt.shape
    bs, ts, _ = x_sample.shape
    n_mem = mem_prompt.shape[1]
    row = lambda g: g.reshape(1, -1)
    weights = _prep_weights(w_in[0], w_conv_out[0], w_uq[0], w_ukv[0], w_mla_out[0], w_mix_out[0],
                            w_qm[0], w_om[0], w_up[0], w_down[0])
    gains = (row(g_mix[0]), row(g_q[0]), row(g_kv[0]), row(g_mem_q[0]), row(g_mlp[0]))
    g_fin = row(g_final)

    mk, mv = _mem_kv(mem_prompt, row(g_mem_kv[0]), w_km[0].astype(BF16), w_vm[0].astype(BF16))
    y_p, conv_p, ckv_p, kr_p = _layer(
        x_prompt, jnp.zeros((bp, CONV_WIDTH - 1, d), F32), jnp.zeros((bp, 0, KV_LORA), F32),
        jnp.zeros((bp, 0, QK_ROPE), F32), mk, mv, weights, gains, w_conv[0], g_fin,
        nb=1, tm=512, nh=4, tq=512, tk=512, tm_mlp=512)
    past = cache_ckv.shape[2]
    y_s, conv_s, ckv_s, kr_s = _layer(
        x_sample, cache_conv[0], cache_ckv[0], cache_krope[0],
        cache_mem_k[0].reshape(bs, n_mem, d), cache_mem_v[0].reshape(bs, n_mem, d),
        weights, gains, w_conv[0], g_fin,
        nb=bs, tm=ts, nh=8, tq=ts, tk=past, tm_mlp=bs * ts)
    mshape = (1, bp, n_mem, MEM_HEADS, d // MEM_HEADS)
    return (y_p, y_s, conv_p[None], ckv_p[None], kr_p[None], mk.reshape(mshape), mv.reshape(mshape),
            conv_s[None], ckv_s[None], kr_s[None])
```

```python
import functools

import jax
import jax.numpy as jnp
from jax import lax
from jax.experimental import pallas as pl
from jax.experimental.pallas import tpu as pltpu

CHUNK = 64
CONV_WIDTH = 3
N_HEADS = 16
QK_NOPE = 64
QK_ROPE = 32
V_HEAD = 64
Q_LORA = 512
KV_LORA = 256
ROPE_THETA = 10000.0
MEM_HEADS = 4
EPS = 1e-6
ATTN_SCALE = (QK_NOPE + QK_ROPE) ** -0.5
LOG2_E = 1.4426950408889634

LANES = 128
SUBLANES = 8
HEAD_PAD = LANES
HALF_ROPE = QK_ROPE // 2
MXU_COLS = 256
HEADS_PER_DOT = MXU_COLS // HEAD_PAD
VMEM_LIMIT = 56 * 1024 * 1024

BF16 = jnp.bfloat16
F32 = jnp.float32


def _dot(a, b):
    return jnp.dot(a, b, preferred_element_type=F32)


def _dot_nt(a, b):
    return lax.dot_general(a, b, (((1,), (1,)), ((), ())), preferred_element_type=F32)


def _rms(x, g):
    return x * lax.rsqrt(jnp.mean(x * x, axis=-1, keepdims=True) + EPS) * g


def _rope(x, cos_t, sin_a, sin_b):
    return (x * cos_t + pltpu.roll(x, HALF_ROPE, 1) * sin_a
            + pltpu.roll(x, LANES - HALF_ROPE, 1) * sin_b)


def _full(shape):
    return pl.BlockSpec(shape, lambda *_: (0,) * len(shape))


def _params(sem):
    return pltpu.CompilerParams(dimension_semantics=sem, vmem_limit_bytes=VMEM_LIMIT)


def _expand_kv(c, kr, wuk_ref, wuv_ref, store_k, store_v):
    ones = (lax.broadcasted_iota(jnp.int32, kr.shape, 1) >= V_HEAD).astype(F32)
    for h in range(0, N_HEADS, HEADS_PER_DOT):
        cols = slice(h * HEAD_PAD, (h + HEADS_PER_DOT) * HEAD_PAD)
        kk = _dot(c, wuk_ref[:, cols])
        vv = _dot(c, wuv_ref[:, cols])
        for e in range(HEADS_PER_DOT):
            lanes = slice(e * HEAD_PAD, (e + 1) * HEAD_PAD)
            store_k(h + e, (kk[:, lanes] + kr).astype(BF16))
            store_v(h + e, (vv[:, lanes] + ones).astype(BF16))


def _in_proj_kernel(x_ref, cache_ref, gmix_ref, win_ref, wconv_ref, wco_ref, gq_ref, wuq_ref,
                    gkv_ref, qtab_ref, ktab_ref, *rest, nb, tm, d, fuse_kv):
    if fuse_kv:
        (wuk_ref, wuv_ref, ga_ref, sgm_ref, q_ref, ckv_ref, kr_ref, k_ref, v_ref, newconv_ref,
         vbuf_ref) = rest
    else:
        ga_ref, sgm_ref, q_ref, ckv_ref, kr_ref, krpad_ref, newconv_ref, vbuf_ref = rest
    j = pl.program_id(1)
    m = nb * tm
    c_u, c_b, c_c = 0, d, 2 * d
    c_q = 3 * d
    c_kv = c_q + Q_LORA
    c_kr = c_kv + KV_LORA
    c_ac = c_kr + HEAD_PAD
    c_am = c_ac + d

    x = x_ref[...].reshape(m, d)
    n = _rms(x, gmix_ref[...]).astype(BF16)

    u = _dot(n, win_ref[:, c_u:c_u + d])
    gc = _dot(n, win_ref[:, c_c:c_c + d])
    v = gc * u
    w0 = wconv_ref[0:1, :]
    w1 = wconv_ref[1:2, :]
    w2 = wconv_ref[2:3, :]
    ys = []
    for i in range(nb):
        vi = v[i * tm:(i + 1) * tm]

        @pl.when(j == 0)
        def _():
            vbuf_ref[i, SUBLANES - 2:SUBLANES, :] = cache_ref[i]

        vbuf_ref[i, SUBLANES:SUBLANES + tm, :] = vi
        ys.append(vbuf_ref[i, SUBLANES - 2:SUBLANES - 2 + tm, :] * w0
                  + vbuf_ref[i, SUBLANES - 1:SUBLANES - 1 + tm, :] * w1
                  + vi * w2)
        tail = vbuf_ref[i, SUBLANES + tm - 2:SUBLANES + tm, :]
        vbuf_ref[i, SUBLANES - 2:SUBLANES, :] = tail
        newconv_ref[i] = tail
    y = ys[0] if nb == 1 else jnp.concatenate(ys, axis=0)
    gb = _dot(n, win_ref[:, c_b:c_b + d])
    ya = _dot((gb * y).astype(BF16), wco_ref[...])
    ac = _dot(n, win_ref[:, c_ac:c_ac + d])
    ga_ref[...] = (jax.nn.sigmoid(ac) * ya).reshape(nb, tm, d)
    am = _dot(n, win_ref[:, c_am:c_am + d])
    sgm_ref[...] = jax.nn.sigmoid(am).reshape(nb, tm, d)

    cq = _dot(n, win_ref[:, c_q:c_q + Q_LORA])
    cqn = _rms(cq, gq_ref[...]).astype(BF16)
    qc, qa, qb = qtab_ref[0], qtab_ref[1], qtab_ref[2]
    for h in range(0, N_HEADS, HEADS_PER_DOT):
        qq = _dot(cqn, wuq_ref[:, h * HEAD_PAD:(h + HEADS_PER_DOT) * HEAD_PAD])
        for e in range(HEADS_PER_DOT):
            qh = _rope(qq[:, e * HEAD_PAD:(e + 1) * HEAD_PAD], qc, qa, qb).astype(BF16)
            for i in range(nb):
                q_ref[i, h + e] = qh[i * tm:(i + 1) * tm]

    ckv = _rms(_dot(n, win_ref[:, c_kv:c_kv + KV_LORA]), gkv_ref[...])
    ckv_ref[...] = ckv.reshape(nb, tm, KV_LORA)
    kr = _dot(n, win_ref[:, c_kr:c_kr + HEAD_PAD])
    kr = _rope(kr, ktab_ref[0], ktab_ref[1], ktab_ref[2])
    kr_ref[...] = kr[:, :QK_ROPE].reshape(nb, tm, QK_ROPE)
    if fuse_kv:
        def store_k(h, kh):
            k_ref[:, h] = kh.reshape(nb, tm, HEAD_PAD)

        def store_v(h, vh):
            v_ref[:, h] = vh.reshape(nb, tm, HEAD_PAD)

        _expand_kv(ckv.astype(BF16), kr, wuk_ref, wuv_ref, store_k, store_v)
    else:
        krpad_ref[...] = kr.astype(BF16).reshape(nb, tm, HEAD_PAD)


def _in_proj(x, conv_cache, g_mix, w_in_p, w_conv, w_co, g_q, w_uq_p, g_kv, qtab, ktab,
             w_uk_p=None, w_uv_p=None, *, nb, tm):
    fuse_kv = w_uk_p is not None
    b, t, d = x.shape
    m = nb * tm
    ncol = w_in_p.shape[1]
    grid = (b // nb, t // tm)
    row = lambda bi, j: (bi, j, 0)
    head_shape = jax.ShapeDtypeStruct((b, N_HEADS, t, HEAD_PAD), BF16)
    head_spec = pl.BlockSpec((nb, N_HEADS, tm, HEAD_PAD), lambda bi, j: (bi, 0, j, 0))
    if fuse_kv:
        kv_shapes, kv_specs = (head_shape, head_shape), (head_spec, head_spec)
        kv_weights = (w_uk_p, w_uv_p)
    else:
        kv_shapes = (jax.ShapeDtypeStruct((b, t, HEAD_PAD), BF16),)
        kv_specs = (pl.BlockSpec((nb, tm, HEAD_PAD), row),)
        kv_weights = ()
    out_shape = (
        jax.ShapeDtypeStruct((b, t, d), F32),
        jax.ShapeDtypeStruct((b, t, d), F32),
        head_shape,
        jax.ShapeDtypeStruct((b, t, KV_LORA), F32),
        jax.ShapeDtypeStruct((b, t, QK_ROPE), F32),
        *kv_shapes,
        jax.ShapeDtypeStruct((b, CONV_WIDTH - 1, d), F32),
    )
    in_specs = [
        pl.BlockSpec((nb, tm, d), row),
        pl.BlockSpec((nb, CONV_WIDTH - 1, d), lambda bi, j: (bi, 0, 0)),
        _full((1, d)),
        _full((d, ncol)),
        _full((CONV_WIDTH, d)),
        _full((d, d)),
        _full((1, Q_LORA)),
        _full((Q_LORA, N_HEADS * HEAD_PAD)),
        _full((1, KV_LORA)),
        pl.BlockSpec((3, m, LANES), lambda bi, j: (0, j, 0)),
        pl.BlockSpec((3, m, LANES), lambda bi, j: (0, j, 0)),
        *[_full(w.shape) for w in kv_weights],
    ]
    out_specs = (
        pl.BlockSpec((nb, tm, d), row),
        pl.BlockSpec((nb, tm, d), row),
        head_spec,
        pl.BlockSpec((nb, tm, KV_LORA), row),
        pl.BlockSpec((nb, tm, QK_ROPE), row),
        *kv_specs,
        pl.BlockSpec((nb, CONV_WIDTH - 1, d), lambda bi, j: (bi, 0, 0)),
    )
    return pl.pallas_call(
        functools.partial(_in_proj_kernel, nb=nb, tm=tm, d=d, fuse_kv=fuse_kv),
        grid=grid, in_specs=in_specs, out_specs=out_specs, out_shape=out_shape,
        scratch_shapes=[pltpu.VMEM((nb, tm + SUBLANES, d), F32)],
        compiler_params=_params(("arbitrary", "arbitrary")),
        name="in_proj",
    )(x, conv_cache, g_mix, w_in_p, w_conv, w_co, g_q, w_uq_p, g_kv, qtab, ktab, *kv_weights)


def _kv_expand_kernel(ckv_ref, krpad_ref, wuk_ref, wuv_ref, k_ref, v_ref):
    def store_k(h, kh):
        k_ref[0, h] = kh

    def store_v(h, vh):
        v_ref[0, h] = vh

    _expand_kv(ckv_ref[0].astype(BF16), krpad_ref[0].astype(F32), wuk_ref, wuv_ref, store_k, store_v)


def _kv_expand(ckv, kr_pad, w_uk_p, w_uv_p, *, tl):
    b, l, _ = ckv.shape
    head_tiles = pl.BlockSpec((1, N_HEADS, tl, HEAD_PAD), lambda bi, j: (bi, 0, j, 0))
    return pl.pallas_call(
        _kv_expand_kernel,
        grid=(b, l // tl),
        in_specs=[
            pl.BlockSpec((1, tl, KV_LORA), lambda bi, j: (bi, j, 0)),
            pl.BlockSpec((1, tl, HEAD_PAD), lambda bi, j: (bi, j, 0)),
            _full((KV_LORA, N_HEADS * HEAD_PAD)),
            _full((KV_LORA, N_HEADS * HEAD_PAD)),
        ],
        out_specs=(head_tiles, head_tiles),
        out_shape=(jax.ShapeDtypeStruct((b, N_HEADS, l, HEAD_PAD), BF16),) * 2,
        compiler_params=_params(("arbitrary", "arbitrary")),
        name="kv_expand",
    )(ckv, kr_pad, w_uk_p, w_uv_p)


def _attention_kernel(q_ref, k_ref, v_ref, o_ref, m_ref, acc_ref, *, nh, tq, tk, past):
    i = pl.program_id(2)
    m_ref[...] = jnp.full(m_ref.shape, -jnp.inf, F32)
    acc_ref[...] = jnp.zeros(acc_ref.shape, F32)

    def update(start, width, mask=None):
        for e in range(nh):
            s = _dot_nt(q_ref[0, e], k_ref[0, e, pl.ds(start, width), :])
            if mask is not None:
                s = jnp.where(mask, s, -jnp.inf)
            m_old = m_ref[e]
            m_new = jnp.maximum(m_old, jnp.max(s, axis=-1, keepdims=True))
            m_wide = m_new[:, :width] if width < LANES else jnp.tile(m_new, (1, width // LANES))
            p = jnp.exp2(s - m_wide).astype(BF16)
            acc_ref[e] = (jnp.exp2(m_old - m_new) * acc_ref[e]
                          + _dot(p, v_ref[0, e, pl.ds(start, width), :]))
            m_ref[e] = m_new

    def full_blocks(n_blocks):
        def body(kb, carry):
            for u in range(n_blocks):
                update(pl.multiple_of((kb * n_blocks + u) * tk, tk), tk)
            return carry
        return body

    n_full = (past + i * tq) // tk
    lax.fori_loop(0, n_full // 2, full_blocks(2), 0)
    lax.fori_loop(n_full - n_full % 2, n_full, full_blocks(1), 0)
    rq = lax.broadcasted_iota(jnp.int32, (tq, tq), 0) // CHUNK
    ck = lax.broadcasted_iota(jnp.int32, (tq, tq), 1) // CHUNK
    update(pl.multiple_of(past + i * tq, tq), tq, ck <= rq)
    first = lax.broadcasted_iota(jnp.int32, (tq, HEAD_PAD), 1) < V_HEAD
    for pair in range(nh // 2):
        acc0, acc1 = acc_ref[2 * pair], acc_ref[2 * pair + 1]
        num = jnp.where(first, acc0, pltpu.roll(acc1, V_HEAD, 1))
        den = jnp.where(first, pltpu.roll(acc0, V_HEAD, 1), acc1)
        o_ref[0, :, pair * HEAD_PAD:(pair + 1) * HEAD_PAD] = (num / den).astype(o_ref.dtype)


def _attention(q, k, v, *, nh, tq, tk):
    b, _, t, _ = q.shape
    l = k.shape[2]
    past = l - t
    assert nh % 2 == 0 and N_HEADS % nh == 0
    assert tq % CHUNK == 0 and all((past + i * tq) % tk == 0 for i in range(t // tq))
    kv_spec = pl.BlockSpec((1, nh, l, HEAD_PAD), lambda bi, hp, i: (bi, hp, 0, 0))
    return pl.pallas_call(
        functools.partial(_attention_kernel, nh=nh, tq=tq, tk=tk, past=past),
        grid=(b, N_HEADS // nh, t // tq),
        in_specs=[
            pl.BlockSpec((1, nh, tq, HEAD_PAD), lambda bi, hp, i: (bi, hp, i, 0)),
            kv_spec, kv_spec,
        ],
        out_specs=pl.BlockSpec((1, tq, nh * V_HEAD), lambda bi, hp, i: (bi, i, hp)),
        out_shape=jax.ShapeDtypeStruct((b, t, N_HEADS * V_HEAD), BF16),
        scratch_shapes=[pltpu.VMEM((nh, tq, LANES), F32), pltpu.VMEM((nh, tq, HEAD_PAD), F32)],
        compiler_params=_params(("arbitrary", "arbitrary", "arbitrary")),
        name="attention",
    )(q, k, v)


def _post_attn_kernel(x_ref, o_ref, ga_ref, sgm_ref, wmo_ref, wmix_ref, gmq_ref, wqm_ref,
                      mk_ref, mv_ref, wom_ref, x2_ref, om_ref, *, nb, tm, d):
    m = nb * tm
    hd = d // MEM_HEADS
    mem_scale = hd ** -0.5
    x = x_ref[...].reshape(m, d)
    yb = _dot(o_ref[...].reshape(m, d), wmo_ref[...])
    mix = ga_ref[...].reshape(m, d) + sgm_ref[...].reshape(m, d) * yb
    x1 = x + _dot(mix.astype(BF16), wmix_ref[...])
    hq = _rms(x1, gmq_ref[...]).astype(BF16)
    qm = (_dot(hq, wqm_ref[...]) * mem_scale).astype(BF16)
    for i in range(nb):
        for h in range(MEM_HEADS):
            cols = slice(h * hd, (h + 1) * hd)
            s = _dot_nt(qm[i * tm:(i + 1) * tm, cols], mk_ref[i, :, cols])
            p = jnp.exp(s - jnp.max(s, axis=-1, keepdims=True))
            l = jnp.sum(p, axis=-1, keepdims=True)
            oh = _dot(p.astype(BF16), mv_ref[i, :, cols]) / l
            om_ref[i * tm:(i + 1) * tm, cols] = oh.astype(BF16)
    x2 = x1 + _dot(om_ref[...], wom_ref[...])
    x2_ref[...] = x2.reshape(nb, tm, d)


def _post_attn(x, o, ga, sgm, w_mo, w_mix, g_mq, w_qm, mem_k, mem_v, w_om, *, nb, tm):
    b, t, d = x.shape
    mem_spec = pl.BlockSpec((nb,) + mem_k.shape[1:], lambda bi, j: (bi, 0, 0))
    row = lambda bi, j: (bi, j, 0)
    return pl.pallas_call(
        functools.partial(_post_attn_kernel, nb=nb, tm=tm, d=d),
        grid=(b // nb, t // tm),
        in_specs=[
            pl.BlockSpec((nb, tm, d), row),
            pl.BlockSpec((nb, tm, d), row),
            pl.BlockSpec((nb, tm, d), row),
            pl.BlockSpec((nb, tm, d), row),
            _full((d, d)), _full((d, d)), _full((1, d)), _full((d, d)),
            mem_spec, mem_spec,
            _full((d, d)),
        ],
        out_specs=pl.BlockSpec((nb, tm, d), row),
        out_shape=jax.ShapeDtypeStruct((b, t, d), F32),
        scratch_shapes=[pltpu.VMEM((nb * tm, d), BF16)],
        compiler_params=_params(("arbitrary", "arbitrary")),
        name="post_attn",
    )(x, o, ga, sgm, w_mo, w_mix, g_mq, w_qm, mem_k, mem_v, w_om)


def _mlp_kernel(x_ref, gmlp_ref, wup_ref, wdown_ref, gfin_ref, y_ref, *, fchunk):
    x = x_ref[...]
    hm = _rms(x, gmlp_ref[...]).astype(BF16)
    acc = x
    for c in range(wup_ref.shape[1] // fchunk):
        cols = slice(c * fchunk, (c + 1) * fchunk)
        hcol = jnp.maximum(_dot(hm, wup_ref[:, cols]), 0.0)
        acc = acc + _dot((hcol * hcol).astype(BF16), wdown_ref[cols, :])
    y_ref[...] = _rms(acc, gfin_ref[...])


def _mlp(x, g_mlp, w_up, w_down, g_final, *, tm):
    n, d = x.shape
    f = w_up.shape[1]
    return pl.pallas_call(
        functools.partial(_mlp_kernel, fchunk=min(f, 1024)),
        grid=(n // tm,),
        in_specs=[
            pl.BlockSpec((tm, d), lambda r: (r, 0)),
            _full((1, d)), _full((d, f)), _full((f, d)), _full((1, d)),
        ],
        out_specs=pl.BlockSpec((tm, d), lambda r: (r, 0)),
        out_shape=jax.ShapeDtypeStruct((n, d), F32),
        compiler_params=_params(("arbitrary",)),
        name="mlp",
    )(x, g_mlp, w_up, w_down, g_final)


def _mem_kv_kernel(mem_ref, g_ref, wk_ref, wv_ref, k_ref, v_ref, kb_ref, vb_ref):
    mn = _rms(mem_ref[0], g_ref[...]).astype(BF16)
    hd = k_ref.shape[-1]
    for w_ref, out_ref, mxu_ref in ((wk_ref, k_ref, kb_ref), (wv_ref, v_ref, vb_ref)):
        full = _dot(mn, w_ref[...])
        mxu_ref[0] = full.astype(BF16)
        for h in range(MEM_HEADS):
            out_ref[0, :, h, :] = full[:, h * hd:(h + 1) * hd]


def _mem_kv(mem, g, w_k, w_v):
    b, n_mem, d = mem.shape
    hd = d // MEM_HEADS
    blk = pl.BlockSpec((1, n_mem, d), lambda bi: (bi, 0, 0))
    out_blk = pl.BlockSpec((1, n_mem, MEM_HEADS, hd), lambda bi: (bi, 0, 0, 0))
    return pl.pallas_call(
        _mem_kv_kernel,
        grid=(b,),
        in_specs=[blk, _full((1, d)), _full((d, d)), _full((d, d))],
        out_specs=(out_blk, out_blk, blk, blk),
        out_shape=((jax.ShapeDtypeStruct((b, n_mem, MEM_HEADS, hd), F32),) * 2
                   + (jax.ShapeDtypeStruct((b, n_mem, d), BF16),) * 2),
        compiler_params=_params(("arbitrary",)),
        name="mem_kv",
    )(mem, g, w_k, w_v)


def _rope_tables(pos, scale):
    t = pos.shape[0]
    inv = ROPE_THETA ** (-jnp.arange(HALF_ROPE, dtype=F32) / HALF_ROPE)
    ang = pos.astype(F32)[:, None] * inv[None, :]
    cos, sin = jnp.cos(ang), jnp.sin(ang)
    z = lambda w: jnp.zeros((t, w), F32)
    cos_t = jnp.concatenate([cos, cos, jnp.ones((t, LANES - QK_ROPE), F32)], axis=1)
    sin_a = jnp.concatenate([z(HALF_ROPE), sin, z(LANES - QK_ROPE)], axis=1)
    sin_b = jnp.concatenate([-sin, z(LANES - HALF_ROPE)], axis=1)
    return jnp.stack([cos_t, sin_a, sin_b]) * scale


def _prep_weights(w_in, w_conv_out, w_uq, w_ukv, w_mla_out, w_mix_out, w_qm, w_om, w_up, w_down):
    d = w_in.shape[0]
    c_kr = 3 * d + Q_LORA + KV_LORA
    w_in_p = jnp.concatenate(
        [w_in[:, :c_kr], w_in[:, c_kr:c_kr + QK_ROPE],
         jnp.zeros((d, HEAD_PAD - QK_ROPE), w_in.dtype), w_in[:, c_kr + QK_ROPE:]], axis=1)
    pad = HEAD_PAD - QK_ROPE - QK_NOPE
    wq = w_uq.reshape(Q_LORA, N_HEADS, QK_NOPE + QK_ROPE)
    w_uq_p = jnp.concatenate(
        [wq[..., QK_NOPE:], wq[..., :QK_NOPE], jnp.zeros((Q_LORA, N_HEADS, pad), w_uq.dtype)],
        axis=-1).reshape(Q_LORA, N_HEADS * HEAD_PAD)
    wkv = w_ukv.reshape(KV_LORA, N_HEADS, QK_NOPE + V_HEAD)
    w_uk_p = jnp.concatenate(
        [jnp.zeros((KV_LORA, N_HEADS, QK_ROPE), w_ukv.dtype), wkv[..., :QK_NOPE],
         jnp.zeros((KV_LORA, N_HEADS, pad), w_ukv.dtype)], axis=-1).reshape(KV_LORA, N_HEADS * HEAD_PAD)
    w_uv_p = jnp.concatenate(
        [wkv[..., QK_NOPE:], jnp.zeros((KV_LORA, N_HEADS, HEAD_PAD - V_HEAD), w_ukv.dtype)],
        axis=-1).reshape(KV_LORA, N_HEADS * HEAD_PAD)
    cast = lambda w: w.astype(BF16)
    return tuple(map(cast, (w_in_p, w_conv_out, w_uq_p, w_uk_p, w_uv_p, w_mla_out, w_mix_out,
                            w_qm, w_om, w_up, w_down)))


def _layer(x, conv_cache, past_ckv, past_kr, mem_k, mem_v, weights, gains, w_conv, g_final,
           *, nb, tm, nh, tq, tk, tm_mlp):
    (w_in_p, w_co, w_uq_p, w_uk_p, w_uv_p, w_mo, w_mix, w_qm, w_om, w_up, w_down) = weights
    g_mix, g_q, g_kv, g_mem_q, g_mlp = gains
    b, t, d = x.shape
    past = past_ckv.shape[1]
    pos = past + jnp.arange(t, dtype=jnp.int32)
    qtab = jnp.tile(_rope_tables(pos, ATTN_SCALE * LOG2_E), (1, nb, 1))
    ktab = jnp.tile(_rope_tables(pos, 1.0), (1, nb, 1))
    proj_args = (x, conv_cache, g_mix, w_in_p, w_conv, w_co, g_q, w_uq_p, g_kv, qtab, ktab)
    if past:
        ga, sgm, q, ckv, kr, kr_pad, new_conv = _in_proj(*proj_args, nb=nb, tm=tm)
        past_kr_pad = jnp.pad(past_kr, ((0, 0), (0, 0), (0, HEAD_PAD - QK_ROPE))).astype(BF16)
        k, v = _kv_expand(jnp.concatenate([past_ckv, ckv], axis=1),
                          jnp.concatenate([past_kr_pad, kr_pad], axis=1), w_uk_p, w_uv_p, tl=past + t)
    else:
        ga, sgm, q, ckv, kr, k, v, new_conv = _in_proj(*proj_args, w_uk_p, w_uv_p, nb=nb, tm=tm)
    o = _attention(q, k, v, nh=nh, tq=tq, tk=tk)
    x2 = _post_attn(x, o, ga, sgm, w_mo, w_mix, g_mem_q, w_qm, mem_k, mem_v, w_om, nb=nb, tm=tm)
    y = _mlp(x2.reshape(b * t, d), g_mlp, w_up, w_down, g_final, tm=tm_mlp).reshape(b, t, d)
    return y, new_conv, ckv, kr


def kernel(x_prompt, x_sample, cache_conv, cache_ckv, cache_krope, cache_mem_k, cache_mem_v, mem_prompt, g_mix, w_in, w_conv, w_conv_out, g_q, w_uq, g_kv, w_ukv, w_mla_out, w_mix_out, g_mem_q, g_mem_kv, w_qm, w_km, w_vm, w_om, g_mlp, w_up, w_down, g_final):
    depth = w_in.shape[0]
    assert depth == 1, "single-layer step"
    bp, tp, d = x_prompt.shape
    bs, ts, _ = x_sample.shape
    n_mem = mem_prompt.shape[1]
    row = lambda g: g.reshape(1, -1)
    weights = _prep_weights(w_in[0], w_conv_out[0], w_uq[0], w_ukv[0], w_mla_out[0], w_mix_out[0],
                            w_qm[0], w_om[0], w_up[0], w_down[0])
    gains = (row(g_mix[0]), row(g_q[0]), row(g_kv[0]), row(g_mem_q[0]), row(g_mlp[0]))
    g_fin = row(g_final)

    mk, mv, mk_mxu, mv_mxu = _mem_kv(mem_prompt, row(g_mem_kv[0]),
                                     w_km[0].astype(BF16), w_vm[0].astype(BF16))
    y_p, conv_p, ckv_p, kr_p = _layer(
        x_prompt, jnp.zeros((bp, CONV_WIDTH - 1, d), F32), jnp.zeros((bp, 0, KV_LORA), F32),
        jnp.zeros((bp, 0, QK_ROPE), F32), mk_mxu, mv_mxu, weights, gains, w_conv[0], g_fin,
        nb=1, tm=512, nh=4, tq=512, tk=512, tm_mlp=512)
    past = cache_ckv.shape[2]
    y_s, conv_s, ckv_s, kr_s = _layer(
        x_sample, cache_conv[0], cache_ckv[0], cache_krope[0],
        cache_mem_k[0].reshape(bs, n_mem, d).astype(BF16),
        cache_mem_v[0].reshape(bs, n_mem, d).astype(BF16),
        weights, gains, w_conv[0], g_fin,
        nb=bs, tm=ts, nh=8, tq=ts, tk=past, tm_mlp=bs * ts)
    return (y_p, y_s, conv_p[None], ckv_p[None], kr_p[None], mk[None], mv[None],
            conv_s[None], ckv_s[None], kr_s[None])
```

```python
import functools

import jax
import jax.numpy as jnp
from jax import lax
from jax.experimental import pallas as pl
from jax.experimental.pallas import tpu as pltpu

CHUNK = 64
CONV_WIDTH = 3
N_HEADS = 16
QK_NOPE = 64
QK_ROPE = 32
V_HEAD = 64
Q_LORA = 512
KV_LORA = 256
ROPE_THETA = 10000.0
MEM_HEADS = 4
EPS = 1e-6
ATTN_SCALE = (QK_NOPE + QK_ROPE) ** -0.5
LOG2_E = 1.4426950408889634

LANES = 128
SUBLANES = 8
HEAD_PAD = LANES
HALF_ROPE = QK_ROPE // 2
MXU_COLS = 256
HEADS_PER_DOT = MXU_COLS // HEAD_PAD
VMEM_LIMIT = 56 * 1024 * 1024

BF16 = jnp.bfloat16
F32 = jnp.float32


def _dot(a, b):
    return jnp.dot(a, b, preferred_element_type=F32)


def _dot_nt(a, b):
    return lax.dot_general(a, b, (((1,), (1,)), ((), ())), preferred_element_type=F32)


def _rms(x, g):
    return x * lax.rsqrt(jnp.mean(x * x, axis=-1, keepdims=True) + EPS) * g


def _rope(x, cos_t, sin_a, sin_b):
    return (x * cos_t + pltpu.roll(x, HALF_ROPE, 1) * sin_a
            + pltpu.roll(x, LANES - HALF_ROPE, 1) * sin_b)


def _full(shape):
    return pl.BlockSpec(shape, lambda *_: (0,) * len(shape))


def _params(sem):
    return pltpu.CompilerParams(dimension_semantics=sem, vmem_limit_bytes=VMEM_LIMIT)


def _expand_kv(c, kr, wuk_ref, wuv_ref, store_k, store_v):
    ones = (lax.broadcasted_iota(jnp.int32, kr.shape, 1) >= V_HEAD).astype(F32)
    for h in range(0, N_HEADS, HEADS_PER_DOT):
        cols = slice(h * HEAD_PAD, (h + HEADS_PER_DOT) * HEAD_PAD)
        kk = _dot(c, wuk_ref[:, cols])
        vv = _dot(c, wuv_ref[:, cols])
        for e in range(HEADS_PER_DOT):
            lanes = slice(e * HEAD_PAD, (e + 1) * HEAD_PAD)
            store_k(h + e, (kk[:, lanes] + kr).astype(BF16))
            store_v(h + e, (vv[:, lanes] + ones).astype(BF16))


def _in_proj_kernel(x_ref, cache_ref, gmix_ref, wa_ref, wb_ref, wconv_ref, wco_ref, gq_ref, wuq_ref,
                    gkv_ref, qtab_ref, ktab_ref, *rest, nb, tm, d, fuse_kv):
    if fuse_kv:
        (wuk_ref, wuv_ref, ga_ref, sgm_ref, q_ref, ckv_ref, kr_ref, k_ref, v_ref, newconv_ref,
         vbuf_ref) = rest
    else:
        ga_ref, sgm_ref, q_ref, ckv_ref, kr_ref, krpad_ref, newconv_ref, vbuf_ref = rest
    j = pl.program_id(1)
    m = nb * tm
    c_u, c_b, c_c = 0, d, 2 * d
    c_q = 3 * d
    c_kv = c_q + Q_LORA
    c_kr, c_ac, c_am = 0, HEAD_PAD, HEAD_PAD + d

    x = x_ref[...].reshape(m, d)
    n = _rms(x, gmix_ref[...]).astype(BF16)

    u = _dot(n, wa_ref[:, c_u:c_u + d])
    gc = _dot(n, wa_ref[:, c_c:c_c + d])
    v = gc * u
    w0 = wconv_ref[0:1, :]
    w1 = wconv_ref[1:2, :]
    w2 = wconv_ref[2:3, :]
    ys = []
    for i in range(nb):
        vi = v[i * tm:(i + 1) * tm]

        @pl.when(j == 0)
        def _():
            vbuf_ref[i, SUBLANES - 2:SUBLANES, :] = cache_ref[i]

        vbuf_ref[i, SUBLANES:SUBLANES + tm, :] = vi
        ys.append(vbuf_ref[i, SUBLANES - 2:SUBLANES - 2 + tm, :] * w0
                  + vbuf_ref[i, SUBLANES - 1:SUBLANES - 1 + tm, :] * w1
                  + vi * w2)
        tail = vbuf_ref[i, SUBLANES + tm - 2:SUBLANES + tm, :]
        vbuf_ref[i, SUBLANES - 2:SUBLANES, :] = tail
        newconv_ref[i] = tail
    y = ys[0] if nb == 1 else jnp.concatenate(ys, axis=0)
    gb = _dot(n, wa_ref[:, c_b:c_b + d])
    ya = _dot((gb * y).astype(BF16), wco_ref[...])
    ac = _dot(n, wb_ref[:, c_ac:c_ac + d])
    ga_ref[...] = (jax.nn.sigmoid(ac) * ya).reshape(nb, tm, d)
    am = _dot(n, wb_ref[:, c_am:c_am + d])
    sgm_ref[...] = jax.nn.sigmoid(am).reshape(nb, tm, d)

    cq = _dot(n, wa_ref[:, c_q:c_q + Q_LORA])
    cqn = _rms(cq, gq_ref[...]).astype(BF16)
    qc, qa, qb = qtab_ref[0], qtab_ref[1], qtab_ref[2]
    for h in range(0, N_HEADS, HEADS_PER_DOT):
        qq = _dot(cqn, wuq_ref[:, h * HEAD_PAD:(h + HEADS_PER_DOT) * HEAD_PAD])
        for e in range(HEADS_PER_DOT):
            qh = _rope(qq[:, e * HEAD_PAD:(e + 1) * HEAD_PAD], qc, qa, qb).astype(BF16)
            for i in range(nb):
                q_ref[i, h + e] = qh[i * tm:(i + 1) * tm]

    ckv = _rms(_dot(n, wa_ref[:, c_kv:c_kv + KV_LORA]), gkv_ref[...])
    ckv_ref[...] = ckv.reshape(nb, tm, KV_LORA)
    kr = _dot(n, wb_ref[:, c_kr:c_kr + HEAD_PAD])
    kr = _rope(kr, ktab_ref[0], ktab_ref[1], ktab_ref[2])
    kr_ref[...] = kr[:, :QK_ROPE].reshape(nb, tm, QK_ROPE)
    if fuse_kv:
        def store_k(h, kh):
            k_ref[:, h] = kh.reshape(nb, tm, HEAD_PAD)

        def store_v(h, vh):
            v_ref[:, h] = vh.reshape(nb, tm, HEAD_PAD)

        _expand_kv(ckv.astype(BF16), kr, wuk_ref, wuv_ref, store_k, store_v)
    else:
        krpad_ref[...] = kr.astype(BF16).reshape(nb, tm, HEAD_PAD)


def _in_proj(x, conv_cache, g_mix, w_in_a, w_in_b, w_conv, w_co, g_q, w_uq_p, g_kv, qtab, ktab,
             w_uk_p=None, w_uv_p=None, *, nb, tm):
    fuse_kv = w_uk_p is not None
    b, t, d = x.shape
    m = nb * tm
    grid = (b // nb, t // tm)
    row = lambda bi, j: (bi, j, 0)
    head_shape = jax.ShapeDtypeStruct((b, N_HEADS, t, HEAD_PAD), BF16)
    head_spec = pl.BlockSpec((nb, N_HEADS, tm, HEAD_PAD), lambda bi, j: (bi, 0, j, 0))
    if fuse_kv:
        kv_shapes, kv_specs = (head_shape, head_shape), (head_spec, head_spec)
        kv_weights = (w_uk_p, w_uv_p)
    else:
        kv_shapes = (jax.ShapeDtypeStruct((b, t, HEAD_PAD), BF16),)
        kv_specs = (pl.BlockSpec((nb, tm, HEAD_PAD), row),)
        kv_weights = ()
    out_shape = (
        jax.ShapeDtypeStruct((b, t, d), F32),
        jax.ShapeDtypeStruct((b, t, d), F32),
        head_shape,
        jax.ShapeDtypeStruct((b, t, KV_LORA), F32),
        jax.ShapeDtypeStruct((b, t, QK_ROPE), F32),
        *kv_shapes,
        jax.ShapeDtypeStruct((b, CONV_WIDTH - 1, d), F32),
    )
    in_specs = [
        pl.BlockSpec((nb, tm, d), row),
        pl.BlockSpec((nb, CONV_WIDTH - 1, d), lambda bi, j: (bi, 0, 0)),
        _full((1, d)),
        _full(w_in_a.shape),
        _full(w_in_b.shape),
        _full((CONV_WIDTH, d)),
        _full((d, d)),
        _full((1, Q_LORA)),
        _full((Q_LORA, N_HEADS * HEAD_PAD)),
        _full((1, KV_LORA)),
        pl.BlockSpec((3, m, LANES), lambda bi, j: (0, j, 0)),
        pl.BlockSpec((3, m, LANES), lambda bi, j: (0, j, 0)),
        *[_full(w.shape) for w in kv_weights],
    ]
    out_specs = (
        pl.BlockSpec((nb, tm, d), row),
        pl.BlockSpec((nb, tm, d), row),
        head_spec,
        pl.BlockSpec((nb, tm, KV_LORA), row),
        pl.BlockSpec((nb, tm, QK_ROPE), row),
        *kv_specs,
        pl.BlockSpec((nb, CONV_WIDTH - 1, d), lambda bi, j: (bi, 0, 0)),
    )
    return pl.pallas_call(
        functools.partial(_in_proj_kernel, nb=nb, tm=tm, d=d, fuse_kv=fuse_kv),
        grid=grid, in_specs=in_specs, out_specs=out_specs, out_shape=out_shape,
        scratch_shapes=[pltpu.VMEM((nb, tm + SUBLANES, d), F32)],
        compiler_params=_params(("arbitrary", "arbitrary")),
        name="in_proj",
    )(x, conv_cache, g_mix, w_in_a, w_in_b, w_conv, w_co, g_q, w_uq_p, g_kv, qtab, ktab, *kv_weights)


def _kv_expand_kernel(ckv_ref, krpad_ref, wuk_ref, wuv_ref, k_ref, v_ref):
    def store_k(h, kh):
        k_ref[0, h] = kh

    def store_v(h, vh):
        v_ref[0, h] = vh

    _expand_kv(ckv_ref[0].astype(BF16), krpad_ref[0].astype(F32), wuk_ref, wuv_ref, store_k, store_v)


def _kv_expand(ckv, kr_pad, w_uk_p, w_uv_p, *, tl):
    b, l, _ = ckv.shape
    head_tiles = pl.BlockSpec((1, N_HEADS, tl, HEAD_PAD), lambda bi, j: (bi, 0, j, 0))
    return pl.pallas_call(
        _kv_expand_kernel,
        grid=(b, l // tl),
        in_specs=[
            pl.BlockSpec((1, tl, KV_LORA), lambda bi, j: (bi, j, 0)),
            pl.BlockSpec((1, tl, HEAD_PAD), lambda bi, j: (bi, j, 0)),
            _full((KV_LORA, N_HEADS * HEAD_PAD)),
            _full((KV_LORA, N_HEADS * HEAD_PAD)),
        ],
        out_specs=(head_tiles, head_tiles),
        out_shape=(jax.ShapeDtypeStruct((b, N_HEADS, l, HEAD_PAD), BF16),) * 2,
        compiler_params=_params(("arbitrary", "arbitrary")),
        name="kv_expand",
    )(ckv, kr_pad, w_uk_p, w_uv_p)


def _attention_kernel(q_ref, k_ref, v_ref, o_ref, m_ref, acc_ref, *, nh, tq, tk, past):
    i = pl.program_id(2)
    m_ref[...] = jnp.full(m_ref.shape, -jnp.inf, F32)
    acc_ref[...] = jnp.zeros(acc_ref.shape, F32)

    def update(start, width, mask=None):
        for e in range(nh):
            s = _dot_nt(q_ref[0, e], k_ref[0, e, pl.ds(start, width), :])
            if mask is not None:
                s = jnp.where(mask, s, -jnp.inf)
            m_old = m_ref[e]
            m_new = jnp.maximum(m_old, jnp.max(s, axis=-1, keepdims=True))
            m_wide = m_new[:, :width] if width < LANES else jnp.tile(m_new, (1, width // LANES))
            p = jnp.exp2(s - m_wide).astype(BF16)
            acc_ref[e] = (jnp.exp2(m_old - m_new) * acc_ref[e]
                          + _dot(p, v_ref[0, e, pl.ds(start, width), :]))
            m_ref[e] = m_new

    def full_blocks(n_blocks):
        def body(kb, carry):
            for u in range(n_blocks):
                update(pl.multiple_of((kb * n_blocks + u) * tk, tk), tk)
            return carry
        return body

    n_full = (past + i * tq) // tk
    lax.fori_loop(0, n_full // 2, full_blocks(2), 0)
    lax.fori_loop(n_full - n_full % 2, n_full, full_blocks(1), 0)
    rq = lax.broadcasted_iota(jnp.int32, (tq, tq), 0) // CHUNK
    ck = lax.broadcasted_iota(jnp.int32, (tq, tq), 1) // CHUNK
    update(pl.multiple_of(past + i * tq, tq), tq, ck <= rq)
    first = lax.broadcasted_iota(jnp.int32, (tq, HEAD_PAD), 1) < V_HEAD
    for pair in range(nh // 2):
        acc0, acc1 = acc_ref[2 * pair], acc_ref[2 * pair + 1]
        num = jnp.where(first, acc0, pltpu.roll(acc1, V_HEAD, 1))
        den = jnp.where(first, pltpu.roll(acc0, V_HEAD, 1), acc1)
        o_ref[0, :, pair * HEAD_PAD:(pair + 1) * HEAD_PAD] = (num / den).astype(o_ref.dtype)


def _attention(q, k, v, *, nh, tq, tk):
    b, _, t, _ = q.shape
    l = k.shape[2]
    past = l - t
    assert nh % 2 == 0 and N_HEADS % nh == 0
    assert tq % CHUNK == 0 and all((past + i * tq) % tk == 0 for i in range(t // tq))
    kv_spec = pl.BlockSpec((1, nh, l, HEAD_PAD), lambda bi, hp, i: (bi, hp, 0, 0))
    return pl.pallas_call(
        functools.partial(_attention_kernel, nh=nh, tq=tq, tk=tk, past=past),
        grid=(b, N_HEADS // nh, t // tq),
        in_specs=[
            pl.BlockSpec((1, nh, tq, HEAD_PAD), lambda bi, hp, i: (bi, hp, i, 0)),
            kv_spec, kv_spec,
        ],
        out_specs=pl.BlockSpec((1, tq, nh * V_HEAD), lambda bi, hp, i: (bi, i, hp)),
        out_shape=jax.ShapeDtypeStruct((b, t, N_HEADS * V_HEAD), BF16),
        scratch_shapes=[pltpu.VMEM((nh, tq, LANES), F32), pltpu.VMEM((nh, tq, HEAD_PAD), F32)],
        compiler_params=_params(("arbitrary", "arbitrary", "arbitrary")),
        name="attention",
    )(q, k, v)


def _post_attn_kernel(x_ref, o_ref, ga_ref, sgm_ref, wmo_ref, wmix_ref, gmq_ref, wqm_ref,
                      mk_ref, mv_ref, wom_ref, x2_ref, om_ref, *, nb, tm, d):
    m = nb * tm
    hd = d // MEM_HEADS
    mem_scale = hd ** -0.5
    x = x_ref[...].reshape(m, d)
    yb = _dot(o_ref[...].reshape(m, d), wmo_ref[...])
    mix = ga_ref[...].reshape(m, d) + sgm_ref[...].reshape(m, d) * yb
    x1 = x + _dot(mix.astype(BF16), wmix_ref[...])
    hq = _rms(x1, gmq_ref[...]).astype(BF16)
    qm = (_dot(hq, wqm_ref[...]) * mem_scale).astype(BF16)
    for i in range(nb):
        for h in range(MEM_HEADS):
            cols = slice(h * hd, (h + 1) * hd)
            s = _dot_nt(qm[i * tm:(i + 1) * tm, cols], mk_ref[i, :, cols])
            p = jnp.exp(s - jnp.max(s, axis=-1, keepdims=True))
            l = jnp.sum(p, axis=-1, keepdims=True)
            oh = _dot(p.astype(BF16), mv_ref[i, :, cols]) / l
            om_ref[i * tm:(i + 1) * tm, cols] = oh.astype(BF16)
    x2 = x1 + _dot(om_ref[...], wom_ref[...])
    x2_ref[...] = x2.reshape(nb, tm, d)


def _post_attn(x, o, ga, sgm, w_mo, w_mix, g_mq, w_qm, mem_k, mem_v, w_om, *, nb, tm):
    b, t, d = x.shape
    mem_spec = pl.BlockSpec((nb,) + mem_k.shape[1:], lambda bi, j: (bi, 0, 0))
    row = lambda bi, j: (bi, j, 0)
    return pl.pallas_call(
        functools.partial(_post_attn_kernel, nb=nb, tm=tm, d=d),
        grid=(b // nb, t // tm),
        in_specs=[
            pl.BlockSpec((nb, tm, d), row),
            pl.BlockSpec((nb, tm, d), row),
            pl.BlockSpec((nb, tm, d), row),
            pl.BlockSpec((nb, tm, d), row),
            _full((d, d)), _full((d, d)), _full((1, d)), _full((d, d)),
            mem_spec, mem_spec,
            _full((d, d)),
        ],
        out_specs=pl.BlockSpec((nb, tm, d), row),
        out_shape=jax.ShapeDtypeStruct((b, t, d), F32),
        scratch_shapes=[pltpu.VMEM((nb * tm, d), BF16)],
        compiler_params=_params(("arbitrary", "arbitrary")),
        name="post_attn",
    )(x, o, ga, sgm, w_mo, w_mix, g_mq, w_qm, mem_k, mem_v, w_om)


def _mlp_kernel(x_ref, gmlp_ref, wup_ref, wdown_ref, gfin_ref, y_ref, *, fchunk):
    x = x_ref[...]
    hm = _rms(x, gmlp_ref[...]).astype(BF16)
    acc = x
    for c in range(wup_ref.shape[1] // fchunk):
        cols = slice(c * fchunk, (c + 1) * fchunk)
        hcol = jnp.maximum(_dot(hm, wup_ref[:, cols]), 0.0)
        acc = acc + _dot((hcol * hcol).astype(BF16), wdown_ref[cols, :])
    y_ref[...] = _rms(acc, gfin_ref[...])


def _mlp(x, g_mlp, w_up, w_down, g_final, *, tm):
    n, d = x.shape
    f = w_up.shape[1]
    return pl.pallas_call(
        functools.partial(_mlp_kernel, fchunk=min(f, 1024)),
        grid=(n // tm,),
        in_specs=[
            pl.BlockSpec((tm, d), lambda r: (r, 0)),
            _full((1, d)), _full((d, f)), _full((f, d)), _full((1, d)),
        ],
        out_specs=pl.BlockSpec((tm, d), lambda r: (r, 0)),
        out_shape=jax.ShapeDtypeStruct((n, d), F32),
        compiler_params=_params(("arbitrary",)),
        name="mlp",
    )(x, g_mlp, w_up, w_down, g_final)


def _mem_kv_kernel(mem_ref, g_ref, wk_ref, wv_ref, k_ref, v_ref, kb_ref, vb_ref):
    mn = _rms(mem_ref[0], g_ref[...]).astype(BF16)
    hd = k_ref.shape[-1]
    for w_ref, out_ref, mxu_ref in ((wk_ref, k_ref, kb_ref), (wv_ref, v_ref, vb_ref)):
        full = _dot(mn, w_ref[...])
        mxu_ref[0] = full.astype(BF16)
        for h in range(MEM_HEADS):
            out_ref[0, :, h, :] = full[:, h * hd:(h + 1) * hd]


def _mem_kv(mem, g, w_k, w_v):
    b, n_mem, d = mem.shape
    hd = d // MEM_HEADS
    blk = pl.BlockSpec((1, n_mem, d), lambda bi: (bi, 0, 0))
    out_blk = pl.BlockSpec((1, n_mem, MEM_HEADS, hd), lambda bi: (bi, 0, 0, 0))
    return pl.pallas_call(
        _mem_kv_kernel,
        grid=(b,),
        in_specs=[blk, _full((1, d)), _full((d, d)), _full((d, d))],
        out_specs=(out_blk, out_blk, blk, blk),
        out_shape=((jax.ShapeDtypeStruct((b, n_mem, MEM_HEADS, hd), F32),) * 2
                   + (jax.ShapeDtypeStruct((b, n_mem, d), BF16),) * 2),
        compiler_params=_params(("arbitrary",)),
        name="mem_kv",
    )(mem, g, w_k, w_v)


def _rope_tables(pos, scale):
    t = pos.shape[0]
    inv = ROPE_THETA ** (-jnp.arange(HALF_ROPE, dtype=F32) / HALF_ROPE)
    ang = pos.astype(F32)[:, None] * inv[None, :]
    cos, sin = jnp.cos(ang), jnp.sin(ang)
    z = lambda w: jnp.zeros((t, w), F32)
    cos_t = jnp.concatenate([cos, cos, jnp.ones((t, LANES - QK_ROPE), F32)], axis=1)
    sin_a = jnp.concatenate([z(HALF_ROPE), sin, z(LANES - QK_ROPE)], axis=1)
    sin_b = jnp.concatenate([-sin, z(LANES - HALF_ROPE)], axis=1)
    return jnp.stack([cos_t, sin_a, sin_b]) * scale


def _prep_weights(w_in, w_conv_out, w_uq, w_ukv, w_mla_out, w_mix_out, w_qm, w_om, w_up, w_down):
    d = w_in.shape[0]
    c_kr = 3 * d + Q_LORA + KV_LORA
    w_in_a = w_in[:, :c_kr]
    w_in_b = jnp.concatenate(
        [w_in[:, c_kr:c_kr + QK_ROPE], jnp.zeros((d, HEAD_PAD - QK_ROPE), w_in.dtype),
         w_in[:, c_kr + QK_ROPE:]], axis=1)
    pad = HEAD_PAD - QK_ROPE - QK_NOPE
    wq = w_uq.reshape(Q_LORA, N_HEADS, QK_NOPE + QK_ROPE)
    w_uq_p = jnp.concatenate(
        [wq[..., QK_NOPE:], wq[..., :QK_NOPE], jnp.zeros((Q_LORA, N_HEADS, pad), w_uq.dtype)],
        axis=-1).reshape(Q_LORA, N_HEADS * HEAD_PAD)
    wkv = w_ukv.reshape(KV_LORA, N_HEADS, QK_NOPE + V_HEAD)
    w_uk_p = jnp.concatenate(
        [jnp.zeros((KV_LORA, N_HEADS, QK_ROPE), w_ukv.dtype), wkv[..., :QK_NOPE],
         jnp.zeros((KV_LORA, N_HEADS, pad), w_ukv.dtype)], axis=-1).reshape(KV_LORA, N_HEADS * HEAD_PAD)
    w_uv_p = jnp.concatenate(
        [wkv[..., QK_NOPE:], jnp.zeros((KV_LORA, N_HEADS, HEAD_PAD - V_HEAD), w_ukv.dtype)],
        axis=-1).reshape(KV_LORA, N_HEADS * HEAD_PAD)
    cast = lambda w: w.astype(BF16)
    return tuple(map(cast, (w_in_a, w_in_b, w_conv_out, w_uq_p, w_uk_p, w_uv_p, w_mla_out, w_mix_out,
                            w_qm, w_om, w_up, w_down)))


def _layer(x, conv_cache, past_ckv, past_kr, mem_k, mem_v, weights, gains, w_conv, g_final,
           *, nb, tm, nh, tq, tk, tm_mlp):
    (w_in_a, w_in_b, w_co, w_uq_p, w_uk_p, w_uv_p, w_mo, w_mix, w_qm, w_om, w_up, w_down) = weights
    g_mix, g_q, g_kv, g_mem_q, g_mlp = gains
    b, t, d = x.shape
    past = past_ckv.shape[1]
    pos = past + jnp.arange(t, dtype=jnp.int32)
    qtab = jnp.tile(_rope_tables(pos, ATTN_SCALE * LOG2_E), (1, nb, 1))
    ktab = jnp.tile(_rope_tables(pos, 1.0), (1, nb, 1))
    proj_args = (x, conv_cache, g_mix, w_in_a, w_in_b, w_conv, w_co, g_q, w_uq_p, g_kv, qtab, ktab)
    if past:
        ga, sgm, q, ckv, kr, kr_pad, new_conv = _in_proj(*proj_args, nb=nb, tm=tm)
        past_kr_pad = jnp.pad(past_kr, ((0, 0), (0, 0), (0, HEAD_PAD - QK_ROPE))).astype(BF16)
        k, v = _kv_expand(jnp.concatenate([past_ckv, ckv], axis=1),
                          jnp.concatenate([past_kr_pad, kr_pad], axis=1), w_uk_p, w_uv_p, tl=past + t)
    else:
        ga, sgm, q, ckv, kr, k, v, new_conv = _in_proj(*proj_args, w_uk_p, w_uv_p, nb=nb, tm=tm)
    o = _attention(q, k, v, nh=nh, tq=tq, tk=tk)
    x2 = _post_attn(x, o, ga, sgm, w_mo, w_mix, g_mem_q, w_qm, mem_k, mem_v, w_om, nb=nb, tm=tm)
    y = _mlp(x2.reshape(b * t, d), g_mlp, w_up, w_down, g_final, tm=tm_mlp).reshape(b, t, d)
    return y, new_conv, ckv, kr


def kernel(x_prompt, x_sample, cache_conv, cache_ckv, cache_krope, cache_mem_k, cache_mem_v, mem_prompt, g_mix, w_in, w_conv, w_conv_out, g_q, w_uq, g_kv, w_ukv, w_mla_out, w_mix_out, g_mem_q, g_mem_kv, w_qm, w_km, w_vm, w_om, g_mlp, w_up, w_down, g_final):
    depth = w_in.shape[0]
    assert depth == 1, "single-layer step"
    bp, tp, d = x_prompt.shape
    bs, ts, _ = x_sample.shape
    n_mem = mem_prompt.shape[1]
    row = lambda g: g.reshape(1, -1)
    weights = _prep_weights(w_in[0], w_conv_out[0], w_uq[0], w_ukv[0], w_mla_out[0], w_mix_out[0],
                            w_qm[0], w_om[0], w_up[0], w_down[0])
    gains = (row(g_mix[0]), row(g_q[0]), row(g_kv[0]), row(g_mem_q[0]), row(g_mlp[0]))
    g_fin = row(g_final)

    mk, mv, mk_mxu, mv_mxu = _mem_kv(mem_prompt, row(g_mem_kv[0]),
                                     w_km[0].astype(BF16), w_vm[0].astype(BF16))
    y_p, conv_p, ckv_p, kr_p = _layer(
        x_prompt, jnp.zeros((bp, CONV_WIDTH - 1, d), F32), jnp.zeros((bp, 0, KV_LORA), F32),
        jnp.zeros((bp, 0, QK_ROPE), F32), mk_mxu, mv_mxu, weights, gains, w_conv[0], g_fin,
        nb=1, tm=512, nh=8, tq=512, tk=512, tm_mlp=512)
    past = cache_ckv.shape[2]
    y_s, conv_s, ckv_s, kr_s = _layer(
        x_sample, cache_conv[0], cache_ckv[0], cache_krope[0],
        cache_mem_k[0].reshape(bs, n_mem, d).astype(BF16),
        cache_mem_v[0].reshape(bs, n_mem, d).astype(BF16),
        weights, gains, w_conv[0], g_fin,
        nb=bs, tm=ts, nh=8, tq=ts, tk=past, tm_mlp=bs * ts)
    return (y_p, y_s, conv_p[None], ckv_p[None], kr_p[None], mk[None], mv[None],
            conv_s[None], ckv_s[None], kr_s[None])
```

```python
import functools

import jax
import jax.numpy as jnp
from jax import lax
from jax.experimental import pallas as pl
from jax.experimental.pallas import tpu as pltpu

CHUNK = 64
CONV_WIDTH = 3
N_HEADS = 16
QK_NOPE = 64
QK_ROPE = 32
V_HEAD = 64
Q_LORA = 512
KV_LORA = 256
ROPE_THETA = 10000.0
MEM_HEADS = 4
EPS = 1e-6
ATTN_SCALE = (QK_NOPE + QK_ROPE) ** -0.5
LOG2_E = 1.4426950408889634

LANES = 128
SUBLANES = 8
HEAD_PAD = LANES
HALF_ROPE = QK_ROPE // 2
MXU_COLS = 256
HEADS_PER_DOT = MXU_COLS // HEAD_PAD
VMEM_LIMIT = 56 * 1024 * 1024

BF16 = jnp.bfloat16
F32 = jnp.float32


def _dot(a, b):
    return jnp.dot(a, b, preferred_element_type=F32)


def _dot_nt(a, b):
    return lax.dot_general(a, b, (((1,), (1,)), ((), ())), preferred_element_type=F32)


def _rms(x, g):
    return x * lax.rsqrt(jnp.mean(x * x, axis=-1, keepdims=True) + EPS) * g


def _rope(x, cos_t, sin_a, sin_b):
    return (x * cos_t + pltpu.roll(x, HALF_ROPE, 1) * sin_a
            + pltpu.roll(x, LANES - HALF_ROPE, 1) * sin_b)


def _full(shape):
    return pl.BlockSpec(shape, lambda *_: (0,) * len(shape))


def _params(sem):
    return pltpu.CompilerParams(dimension_semantics=sem, vmem_limit_bytes=VMEM_LIMIT)


def _expand_kv(c, kr, wuk_ref, wuv_ref, store_k, store_v):
    ones = (lax.broadcasted_iota(jnp.int32, kr.shape, 1) >= V_HEAD).astype(F32)
    for h in range(0, N_HEADS, HEADS_PER_DOT):
        cols = slice(h * HEAD_PAD, (h + HEADS_PER_DOT) * HEAD_PAD)
        kk = _dot(c, wuk_ref[:, cols])
        vv = _dot(c, wuv_ref[:, cols])
        for e in range(HEADS_PER_DOT):
            lanes = slice(e * HEAD_PAD, (e + 1) * HEAD_PAD)
            store_k(h + e, (kk[:, lanes] + kr).astype(BF16))
            store_v(h + e, (vv[:, lanes] + ones).astype(BF16))


def _in_proj_kernel(x_ref, cache_ref, gmix_ref, wa_ref, wb_ref, wconv_ref, wco_ref, gq_ref, wuq_ref,
                    gkv_ref, qtab_ref, ktab_ref, *rest, nb, tm, d, fuse_kv):
    if fuse_kv:
        (wuk_ref, wuv_ref, ga_ref, sgm_ref, q_ref, ckv_ref, kr_ref, k_ref, v_ref, newconv_ref,
         vbuf_ref) = rest
    else:
        ga_ref, sgm_ref, q_ref, ckv_ref, kr_ref, krpad_ref, newconv_ref, vbuf_ref = rest
    j = pl.program_id(1)
    m = nb * tm
    c_u, c_b, c_c = 0, d, 2 * d
    c_q = 3 * d
    c_kv = c_q + Q_LORA
    c_kr, c_ac, c_am = 0, HEAD_PAD, HEAD_PAD + d

    x = x_ref[...].reshape(m, d)
    n = _rms(x, gmix_ref[...]).astype(BF16)

    u = _dot(n, wa_ref[:, c_u:c_u + d])
    gc = _dot(n, wa_ref[:, c_c:c_c + d])
    v = gc * u
    w0 = wconv_ref[0:1, :]
    w1 = wconv_ref[1:2, :]
    w2 = wconv_ref[2:3, :]
    ys = []
    for i in range(nb):
        vi = v[i * tm:(i + 1) * tm]

        @pl.when(j == 0)
        def _():
            vbuf_ref[i, SUBLANES - 2:SUBLANES, :] = cache_ref[i]

        vbuf_ref[i, SUBLANES:SUBLANES + tm, :] = vi
        ys.append(vbuf_ref[i, SUBLANES - 2:SUBLANES - 2 + tm, :] * w0
                  + vbuf_ref[i, SUBLANES - 1:SUBLANES - 1 + tm, :] * w1
                  + vi * w2)
        tail = vbuf_ref[i, SUBLANES + tm - 2:SUBLANES + tm, :]
        vbuf_ref[i, SUBLANES - 2:SUBLANES, :] = tail
        newconv_ref[i] = tail
    y = ys[0] if nb == 1 else jnp.concatenate(ys, axis=0)
    gb = _dot(n, wa_ref[:, c_b:c_b + d])
    ya = _dot((gb * y).astype(BF16), wco_ref[...])
    ac = _dot(n, wb_ref[:, c_ac:c_ac + d])
    ga_ref[...] = (jax.nn.sigmoid(ac) * ya).reshape(nb, tm, d)
    am = _dot(n, wb_ref[:, c_am:c_am + d])
    sgm_ref[...] = jax.nn.sigmoid(am).reshape(nb, tm, d)

    cq = _dot(n, wa_ref[:, c_q:c_q + Q_LORA])
    cqn = _rms(cq, gq_ref[...]).astype(BF16)
    qc, qa, qb = qtab_ref[0], qtab_ref[1], qtab_ref[2]
    for h in range(0, N_HEADS, HEADS_PER_DOT):
        qq = _dot(cqn, wuq_ref[:, h * HEAD_PAD:(h + HEADS_PER_DOT) * HEAD_PAD])
        for e in range(HEADS_PER_DOT):
            qh = _rope(qq[:, e * HEAD_PAD:(e + 1) * HEAD_PAD], qc, qa, qb).astype(BF16)
            for i in range(nb):
                q_ref[i, h + e] = qh[i * tm:(i + 1) * tm]

    ckv = _rms(_dot(n, wa_ref[:, c_kv:c_kv + KV_LORA]), gkv_ref[...])
    ckv_ref[...] = ckv.reshape(nb, tm, KV_LORA)
    kr = _dot(n, wb_ref[:, c_kr:c_kr + HEAD_PAD])
    kr = _rope(kr, ktab_ref[0], ktab_ref[1], ktab_ref[2])
    kr_ref[...] = kr[:, :QK_ROPE].reshape(nb, tm, QK_ROPE)
    if fuse_kv:
        def store_k(h, kh):
            k_ref[:, h] = kh.reshape(nb, tm, HEAD_PAD)

        def store_v(h, vh):
            v_ref[:, h] = vh.reshape(nb, tm, HEAD_PAD)

        _expand_kv(ckv.astype(BF16), kr, wuk_ref, wuv_ref, store_k, store_v)
    else:
        krpad_ref[...] = kr.astype(BF16).reshape(nb, tm, HEAD_PAD)


def _in_proj(x, conv_cache, g_mix, w_in_a, w_in_b, w_conv, w_co, g_q, w_uq_p, g_kv, qtab, ktab,
             w_uk_p=None, w_uv_p=None, *, nb, tm):
    fuse_kv = w_uk_p is not None
    b, t, d = x.shape
    m = nb * tm
    grid = (b // nb, t // tm)
    row = lambda bi, j: (bi, j, 0)
    head_shape = jax.ShapeDtypeStruct((b, N_HEADS, t, HEAD_PAD), BF16)
    head_spec = pl.BlockSpec((nb, N_HEADS, tm, HEAD_PAD), lambda bi, j: (bi, 0, j, 0))
    if fuse_kv:
        kv_shapes, kv_specs = (head_shape, head_shape), (head_spec, head_spec)
        kv_weights = (w_uk_p, w_uv_p)
    else:
        kv_shapes = (jax.ShapeDtypeStruct((b, t, HEAD_PAD), BF16),)
        kv_specs = (pl.BlockSpec((nb, tm, HEAD_PAD), row),)
        kv_weights = ()
    out_shape = (
        jax.ShapeDtypeStruct((b, t, d), F32),
        jax.ShapeDtypeStruct((b, t, d), F32),
        head_shape,
        jax.ShapeDtypeStruct((b, t, KV_LORA), F32),
        jax.ShapeDtypeStruct((b, t, QK_ROPE), F32),
        *kv_shapes,
        jax.ShapeDtypeStruct((b, CONV_WIDTH - 1, d), F32),
    )
    in_specs = [
        pl.BlockSpec((nb, tm, d), row),
        pl.BlockSpec((nb, CONV_WIDTH - 1, d), lambda bi, j: (bi, 0, 0)),
        _full((1, d)),
        _full(w_in_a.shape),
        _full(w_in_b.shape),
        _full((CONV_WIDTH, d)),
        _full((d, d)),
        _full((1, Q_LORA)),
        _full((Q_LORA, N_HEADS * HEAD_PAD)),
        _full((1, KV_LORA)),
        pl.BlockSpec((3, m, LANES), lambda bi, j: (0, j, 0)),
        pl.BlockSpec((3, m, LANES), lambda bi, j: (0, j, 0)),
        *[_full(w.shape) for w in kv_weights],
    ]
    out_specs = (
        pl.BlockSpec((nb, tm, d), row),
        pl.BlockSpec((nb, tm, d), row),
        head_spec,
        pl.BlockSpec((nb, tm, KV_LORA), row),
        pl.BlockSpec((nb, tm, QK_ROPE), row),
        *kv_specs,
        pl.BlockSpec((nb, CONV_WIDTH - 1, d), lambda bi, j: (bi, 0, 0)),
    )
    return pl.pallas_call(
        functools.partial(_in_proj_kernel, nb=nb, tm=tm, d=d, fuse_kv=fuse_kv),
        grid=grid, in_specs=in_specs, out_specs=out_specs, out_shape=out_shape,
        scratch_shapes=[pltpu.VMEM((nb, tm + SUBLANES, d), F32)],
        compiler_params=_params(("arbitrary", "arbitrary")),
        name="in_proj",
    )(x, conv_cache, g_mix, w_in_a, w_in_b, w_conv, w_co, g_q, w_uq_p, g_kv, qtab, ktab, *kv_weights)


def _kv_expand_kernel(ckv_ref, krpad_ref, wuk_ref, wuv_ref, k_ref, v_ref):
    def store_k(h, kh):
        k_ref[0, h] = kh

    def store_v(h, vh):
        v_ref[0, h] = vh

    _expand_kv(ckv_ref[0].astype(BF16), krpad_ref[0].astype(F32), wuk_ref, wuv_ref, store_k, store_v)


def _kv_expand(ckv, kr_pad, w_uk_p, w_uv_p, *, tl):
    b, l, _ = ckv.shape
    head_tiles = pl.BlockSpec((1, N_HEADS, tl, HEAD_PAD), lambda bi, j: (bi, 0, j, 0))
    return pl.pallas_call(
        _kv_expand_kernel,
        grid=(b, l // tl),
        in_specs=[
            pl.BlockSpec((1, tl, KV_LORA), lambda bi, j: (bi, j, 0)),
            pl.BlockSpec((1, tl, HEAD_PAD), lambda bi, j: (bi, j, 0)),
            _full((KV_LORA, N_HEADS * HEAD_PAD)),
            _full((KV_LORA, N_HEADS * HEAD_PAD)),
        ],
        out_specs=(head_tiles, head_tiles),
        out_shape=(jax.ShapeDtypeStruct((b, N_HEADS, l, HEAD_PAD), BF16),) * 2,
        compiler_params=_params(("arbitrary", "arbitrary")),
        name="kv_expand",
    )(ckv, kr_pad, w_uk_p, w_uv_p)


def _attention_kernel(q_ref, k_ref, v_ref, o_ref, m_ref, acc_ref, *, nh, tq, tk, past):
    i = pl.program_id(2)
    m_ref[...] = jnp.full(m_ref.shape, -jnp.inf, F32)
    acc_ref[...] = jnp.zeros(acc_ref.shape, F32)

    def update(start, width, mask=None):
        for e in range(nh):
            s = _dot_nt(q_ref[0, e], k_ref[0, e, pl.ds(start, width), :])
            if mask is not None:
                s = jnp.where(mask, s, -jnp.inf)
            m_old = m_ref[e]
            m_new = jnp.maximum(m_old, jnp.max(s, axis=-1, keepdims=True))
            m_wide = m_new[:, :width] if width < LANES else jnp.tile(m_new, (1, width // LANES))
            p = jnp.exp2(s - m_wide).astype(BF16)
            acc_ref[e] = (jnp.exp2(m_old - m_new) * acc_ref[e]
                          + _dot(p, v_ref[0, e, pl.ds(start, width), :]))
            m_ref[e] = m_new

    def full_blocks(n_blocks):
        def body(kb, carry):
            update(pl.multiple_of(kb * n_blocks * tk, tk), n_blocks * tk)
            return carry
        return body

    n_full = (past + i * tq) // tk
    done = 0
    for n_blocks in (4, 2, 1):
        if n_blocks * tk <= k_ref.shape[2]:
            count = (n_full - done) // n_blocks
            first_span = done // n_blocks
            lax.fori_loop(first_span, first_span + count, full_blocks(n_blocks), 0)
            done = done + count * n_blocks
    rq = lax.broadcasted_iota(jnp.int32, (tq, tq), 0) // CHUNK
    ck = lax.broadcasted_iota(jnp.int32, (tq, tq), 1) // CHUNK
    update(pl.multiple_of(past + i * tq, tq), tq, ck <= rq)
    first = lax.broadcasted_iota(jnp.int32, (tq, HEAD_PAD), 1) < V_HEAD
    for pair in range(nh // 2):
        acc0, acc1 = acc_ref[2 * pair], acc_ref[2 * pair + 1]
        num = jnp.where(first, acc0, pltpu.roll(acc1, V_HEAD, 1))
        den = jnp.where(first, pltpu.roll(acc0, V_HEAD, 1), acc1)
        o_ref[0, :, pair * HEAD_PAD:(pair + 1) * HEAD_PAD] = (num / den).astype(o_ref.dtype)


def _attention(q, k, v, *, nh, tq, tk):
    b, _, t, _ = q.shape
    l = k.shape[2]
    past = l - t
    assert nh % 2 == 0 and N_HEADS % nh == 0
    assert tq % CHUNK == 0 and all((past + i * tq) % tk == 0 for i in range(t // tq))
    kv_spec = pl.BlockSpec((1, nh, l, HEAD_PAD), lambda bi, hp, i: (bi, hp, 0, 0))
    return pl.pallas_call(
        functools.partial(_attention_kernel, nh=nh, tq=tq, tk=tk, past=past),
        grid=(b, N_HEADS // nh, t // tq),
        in_specs=[
            pl.BlockSpec((1, nh, tq, HEAD_PAD), lambda bi, hp, i: (bi, hp, i, 0)),
            kv_spec, kv_spec,
        ],
        out_specs=pl.BlockSpec((1, tq, nh * V_HEAD), lambda bi, hp, i: (bi, i, hp)),
        out_shape=jax.ShapeDtypeStruct((b, t, N_HEADS * V_HEAD), BF16),
        scratch_shapes=[pltpu.VMEM((nh, tq, LANES), F32), pltpu.VMEM((nh, tq, HEAD_PAD), F32)],
        compiler_params=_params(("arbitrary", "arbitrary", "arbitrary")),
        name="attention",
    )(q, k, v)


def _post_attn_kernel(x_ref, o_ref, ga_ref, sgm_ref, wmo_ref, wmix_ref, gmq_ref, wqm_ref,
                      mk_ref, mv_ref, wom_ref, x2_ref, om_ref, *, nb, tm, d):
    m = nb * tm
    hd = d // MEM_HEADS
    mem_scale = hd ** -0.5
    x = x_ref[...].reshape(m, d)
    yb = _dot(o_ref[...].reshape(m, d), wmo_ref[...])
    mix = ga_ref[...].reshape(m, d) + sgm_ref[...].reshape(m, d) * yb
    x1 = x + _dot(mix.astype(BF16), wmix_ref[...])
    hq = _rms(x1, gmq_ref[...]).astype(BF16)
    qm = (_dot(hq, wqm_ref[...]) * mem_scale).astype(BF16)
    for i in range(nb):
        for h in range(MEM_HEADS):
            cols = slice(h * hd, (h + 1) * hd)
            s = _dot_nt(qm[i * tm:(i + 1) * tm, cols], mk_ref[i, :, cols])
            p = jnp.exp(s - jnp.max(s, axis=-1, keepdims=True))
            l = jnp.sum(p, axis=-1, keepdims=True)
            oh = _dot(p.astype(BF16), mv_ref[i, :, cols]) / l
            om_ref[i * tm:(i + 1) * tm, cols] = oh.astype(BF16)
    x2 = x1 + _dot(om_ref[...], wom_ref[...])
    x2_ref[...] = x2.reshape(nb, tm, d)


def _post_attn(x, o, ga, sgm, w_mo, w_mix, g_mq, w_qm, mem_k, mem_v, w_om, *, nb, tm):
    b, t, d = x.shape
    mem_spec = pl.BlockSpec((nb,) + mem_k.shape[1:], lambda bi, j: (bi, 0, 0))
    row = lambda bi, j: (bi, j, 0)
    return pl.pallas_call(
        functools.partial(_post_attn_kernel, nb=nb, tm=tm, d=d),
        grid=(b // nb, t // tm),
        in_specs=[
            pl.BlockSpec((nb, tm, d), row),
            pl.BlockSpec((nb, tm, d), row),
            pl.BlockSpec((nb, tm, d), row),
            pl.BlockSpec((nb, tm, d), row),
            _full((d, d)), _full((d, d)), _full((1, d)), _full((d, d)),
            mem_spec, mem_spec,
            _full((d, d)),
        ],
        out_specs=pl.BlockSpec((nb, tm, d), row),
        out_shape=jax.ShapeDtypeStruct((b, t, d), F32),
        scratch_shapes=[pltpu.VMEM((nb * tm, d), BF16)],
        compiler_params=_params(("arbitrary", "arbitrary")),
        name="post_attn",
    )(x, o, ga, sgm, w_mo, w_mix, g_mq, w_qm, mem_k, mem_v, w_om)


def _mlp_kernel(x_ref, gmlp_ref, wup_ref, wdown_ref, gfin_ref, y_ref, *, fchunk):
    x = x_ref[...]
    hm = _rms(x, gmlp_ref[...]).astype(BF16)
    acc = x
    for c in range(wup_ref.shape[1] // fchunk):
        cols = slice(c * fchunk, (c + 1) * fchunk)
        hcol = jnp.maximum(_dot(hm, wup_ref[:, cols]), 0.0)
        acc = acc + _dot((hcol * hcol).astype(BF16), wdown_ref[cols, :])
    y_ref[...] = _rms(acc, gfin_ref[...])


def _mlp(x, g_mlp, w_up, w_down, g_final, *, tm):
    n, d = x.shape
    f = w_up.shape[1]
    return pl.pallas_call(
        functools.partial(_mlp_kernel, fchunk=min(f, 1024)),
        grid=(n // tm,),
        in_specs=[
            pl.BlockSpec((tm, d), lambda r: (r, 0)),
            _full((1, d)), _full((d, f)), _full((f, d)), _full((1, d)),
        ],
        out_specs=pl.BlockSpec((tm, d), lambda r: (r, 0)),
        out_shape=jax.ShapeDtypeStruct((n, d), F32),
        compiler_params=_params(("arbitrary",)),
        name="mlp",
    )(x, g_mlp, w_up, w_down, g_final)


def _mem_kv_kernel(mem_ref, g_ref, wk_ref, wv_ref, k_ref, v_ref, kb_ref, vb_ref):
    mn = _rms(mem_ref[0], g_ref[...]).astype(BF16)
    hd = k_ref.shape[-1]
    for w_ref, out_ref, mxu_ref in ((wk_ref, k_ref, kb_ref), (wv_ref, v_ref, vb_ref)):
        full = _dot(mn, w_ref[...])
        mxu_ref[0] = full.astype(BF16)
        for h in range(MEM_HEADS):
            out_ref[0, :, h, :] = full[:, h * hd:(h + 1) * hd]


def _mem_kv(mem, g, w_k, w_v):
    b, n_mem, d = mem.shape
    hd = d // MEM_HEADS
    blk = pl.BlockSpec((1, n_mem, d), lambda bi: (bi, 0, 0))
    out_blk = pl.BlockSpec((1, n_mem, MEM_HEADS, hd), lambda bi: (bi, 0, 0, 0))
    return pl.pallas_call(
        _mem_kv_kernel,
        grid=(b,),
        in_specs=[blk, _full((1, d)), _full((d, d)), _full((d, d))],
        out_specs=(out_blk, out_blk, blk, blk),
        out_shape=((jax.ShapeDtypeStruct((b, n_mem, MEM_HEADS, hd), F32),) * 2
                   + (jax.ShapeDtypeStruct((b, n_mem, d), BF16),) * 2),
        compiler_params=_params(("arbitrary",)),
        name="mem_kv",
    )(mem, g, w_k, w_v)


def _rope_tables(pos, scale):
    t = pos.shape[0]
    inv = ROPE_THETA ** (-jnp.arange(HALF_ROPE, dtype=F32) / HALF_ROPE)
    ang = pos.astype(F32)[:, None] * inv[None, :]
    cos, sin = jnp.cos(ang), jnp.sin(ang)
    z = lambda w: jnp.zeros((t, w), F32)
    cos_t = jnp.concatenate([cos, cos, jnp.ones((t, LANES - QK_ROPE), F32)], axis=1)
    sin_a = jnp.concatenate([z(HALF_ROPE), sin, z(LANES - QK_ROPE)], axis=1)
    sin_b = jnp.concatenate([-sin, z(LANES - HALF_ROPE)], axis=1)
    return jnp.stack([cos_t, sin_a, sin_b]) * scale


def _prep_weights(w_in, w_conv_out, w_uq, w_ukv, w_mla_out, w_mix_out, w_qm, w_om, w_up, w_down):
    d = w_in.shape[0]
    c_kr = 3 * d + Q_LORA + KV_LORA
    w_in_a = w_in[:, :c_kr]
    w_in_b = jnp.concatenate(
        [w_in[:, c_kr:c_kr + QK_ROPE], jnp.zeros((d, HEAD_PAD - QK_ROPE), w_in.dtype),
         w_in[:, c_kr + QK_ROPE:]], axis=1)
    pad = HEAD_PAD - QK_ROPE - QK_NOPE
    wq = w_uq.reshape(Q_LORA, N_HEADS, QK_NOPE + QK_ROPE)
    w_uq_p = jnp.concatenate(
        [wq[..., QK_NOPE:], wq[..., :QK_NOPE], jnp.zeros((Q_LORA, N_HEADS, pad), w_uq.dtype)],
        axis=-1).reshape(Q_LORA, N_HEADS * HEAD_PAD)
    wkv = w_ukv.reshape(KV_LORA, N_HEADS, QK_NOPE + V_HEAD)
    w_uk_p = jnp.concatenate(
        [jnp.zeros((KV_LORA, N_HEADS, QK_ROPE), w_ukv.dtype), wkv[..., :QK_NOPE],
         jnp.zeros((KV_LORA, N_HEADS, pad), w_ukv.dtype)], axis=-1).reshape(KV_LORA, N_HEADS * HEAD_PAD)
    w_uv_p = jnp.concatenate(
        [wkv[..., QK_NOPE:], jnp.zeros((KV_LORA, N_HEADS, HEAD_PAD - V_HEAD), w_ukv.dtype)],
        axis=-1).reshape(KV_LORA, N_HEADS * HEAD_PAD)
    cast = lambda w: w.astype(BF16)
    return tuple(map(cast, (w_in_a, w_in_b, w_conv_out, w_uq_p, w_uk_p, w_uv_p, w_mla_out, w_mix_out,
                            w_qm, w_om, w_up, w_down)))


def _layer(x, conv_cache, past_ckv, past_kr, mem_k, mem_v, weights, gains, w_conv, g_final,
           *, nb, tm, nh, tq, tk, tm_mlp):
    (w_in_a, w_in_b, w_co, w_uq_p, w_uk_p, w_uv_p, w_mo, w_mix, w_qm, w_om, w_up, w_down) = weights
    g_mix, g_q, g_kv, g_mem_q, g_mlp = gains
    b, t, d = x.shape
    past = past_ckv.shape[1]
    pos = past + jnp.arange(t, dtype=jnp.int32)
    qtab = jnp.tile(_rope_tables(pos, ATTN_SCALE * LOG2_E), (1, nb, 1))
    ktab = jnp.tile(_rope_tables(pos, 1.0), (1, nb, 1))
    proj_args = (x, conv_cache, g_mix, w_in_a, w_in_b, w_conv, w_co, g_q, w_uq_p, g_kv, qtab, ktab)
    if past:
        ga, sgm, q, ckv, kr, kr_pad, new_conv = _in_proj(*proj_args, nb=nb, tm=tm)
        past_kr_pad = jnp.pad(past_kr, ((0, 0), (0, 0), (0, HEAD_PAD - QK_ROPE))).astype(BF16)
        k, v = _kv_expand(jnp.concatenate([past_ckv, ckv], axis=1),
                          jnp.concatenate([past_kr_pad, kr_pad], axis=1), w_uk_p, w_uv_p, tl=past + t)
    else:
        ga, sgm, q, ckv, kr, k, v, new_conv = _in_proj(*proj_args, w_uk_p, w_uv_p, nb=nb, tm=tm)
    o = _attention(q, k, v, nh=nh, tq=tq, tk=tk)
    x2 = _post_attn(x, o, ga, sgm, w_mo, w_mix, g_mem_q, w_qm, mem_k, mem_v, w_om, nb=nb, tm=tm)
    y = _mlp(x2.reshape(b * t, d), g_mlp, w_up, w_down, g_final, tm=tm_mlp).reshape(b, t, d)
    return y, new_conv, ckv, kr


def kernel(x_prompt, x_sample, cache_conv, cache_ckv, cache_krope, cache_mem_k, cache_mem_v, mem_prompt, g_mix, w_in, w_conv, w_conv_out, g_q, w_uq, g_kv, w_ukv, w_mla_out, w_mix_out, g_mem_q, g_mem_kv, w_qm, w_km, w_vm, w_om, g_mlp, w_up, w_down, g_final):
    depth = w_in.shape[0]
    assert depth == 1, "single-layer step"
    bp, tp, d = x_prompt.shape
    bs, ts, _ = x_sample.shape
    n_mem = mem_prompt.shape[1]
    row = lambda g: g.reshape(1, -1)
    weights = _prep_weights(w_in[0], w_conv_out[0], w_uq[0], w_ukv[0], w_mla_out[0], w_mix_out[0],
                            w_qm[0], w_om[0], w_up[0], w_down[0])
    gains = (row(g_mix[0]), row(g_q[0]), row(g_kv[0]), row(g_mem_q[0]), row(g_mlp[0]))
    g_fin = row(g_final)

    mk, mv, mk_mxu, mv_mxu = _mem_kv(mem_prompt, row(g_mem_kv[0]),
                                     w_km[0].astype(BF16), w_vm[0].astype(BF16))
    y_p, conv_p, ckv_p, kr_p = _layer(
        x_prompt, jnp.zeros((bp, CONV_WIDTH - 1, d), F32), jnp.zeros((bp, 0, KV_LORA), F32),
        jnp.zeros((bp, 0, QK_ROPE), F32), mk_mxu, mv_mxu, weights, gains, w_conv[0], g_fin,
        nb=1, tm=512, nh=8, tq=512, tk=512, tm_mlp=512)
    past = cache_ckv.shape[2]
    y_s, conv_s, ckv_s, kr_s = _layer(
        x_sample, cache_conv[0], cache_ckv[0], cache_krope[0],
        cache_mem_k[0].reshape(bs, n_mem, d).astype(BF16),
        cache_mem_v[0].reshape(bs, n_mem, d).astype(BF16),
        weights, gains, w_conv[0], g_fin,
        nb=bs, tm=ts, nh=8, tq=ts, tk=past, tm_mlp=bs * ts)
    return (y_p, y_s, conv_p[None], ckv_p[None], kr_p[None], mk[None], mv[None],
            conv_s[None], ckv_s[None], kr_s[None])
```

```python
import functools

import jax
import jax.numpy as jnp
from jax import lax
from jax.experimental import pallas as pl
from jax.experimental.pallas import tpu as pltpu

CHUNK = 64
CONV_WIDTH = 3
N_HEADS = 16
QK_NOPE = 64
QK_ROPE = 32
V_HEAD = 64
Q_LORA = 512
KV_LORA = 256
ROPE_THETA = 10000.0
MEM_HEADS = 4
EPS = 1e-6
ATTN_SCALE = (QK_NOPE + QK_ROPE) ** -0.5
LOG2_E = 1.4426950408889634

LANES = 128
SUBLANES = 8
HEAD_PAD = LANES
HALF_ROPE = QK_ROPE // 2
MXU_COLS = 256
HEADS_PER_DOT = MXU_COLS // HEAD_PAD
VMEM_LIMIT = 56 * 1024 * 1024

BF16 = jnp.bfloat16
F32 = jnp.float32


def _dot(a, b):
    return jnp.dot(a, b, preferred_element_type=F32)


def _dot_nt(a, b):
    return lax.dot_general(a, b, (((1,), (1,)), ((), ())), preferred_element_type=F32)


def _rms(x, g):
    return x * lax.rsqrt(jnp.mean(x * x, axis=-1, keepdims=True) + EPS) * g


def _rope(x, cos_t, sin_a, sin_b):
    return (x * cos_t + pltpu.roll(x, HALF_ROPE, 1) * sin_a
            + pltpu.roll(x, LANES - HALF_ROPE, 1) * sin_b)


def _full(shape):
    return pl.BlockSpec(shape, lambda *_: (0,) * len(shape))


def _params(sem):
    return pltpu.CompilerParams(dimension_semantics=sem, vmem_limit_bytes=VMEM_LIMIT)


def _expand_kv(c, kr, wuk_ref, wuv_ref, store_k, store_v):
    ones = (lax.broadcasted_iota(jnp.int32, kr.shape, 1) >= V_HEAD).astype(F32)
    for h in range(0, N_HEADS, HEADS_PER_DOT):
        cols = slice(h * HEAD_PAD, (h + HEADS_PER_DOT) * HEAD_PAD)
        kk = _dot(c, wuk_ref[:, cols])
        vv = _dot(c, wuv_ref[:, cols])
        for e in range(HEADS_PER_DOT):
            lanes = slice(e * HEAD_PAD, (e + 1) * HEAD_PAD)
            store_k(h + e, (kk[:, lanes] + kr).astype(BF16))
            store_v(h + e, (vv[:, lanes] + ones).astype(BF16))


def _in_proj_kernel(x_ref, cache_ref, gmix_ref, wa_ref, wb_ref, wconv_ref, wco_ref, gq_ref, wuq_ref,
                    gkv_ref, qtab_ref, ktab_ref, *rest, nb, tm, d, fuse_kv):
    if fuse_kv:
        (wuk_ref, wuv_ref, ga_ref, sgm_ref, q_ref, ckv_ref, kr_ref, k_ref, v_ref, newconv_ref,
         vbuf_ref) = rest
    else:
        ga_ref, sgm_ref, q_ref, ckv_ref, kr_ref, krpad_ref, newconv_ref, vbuf_ref = rest
    j = pl.program_id(1)
    m = nb * tm
    c_u, c_b, c_c = 0, d, 2 * d
    c_q = 3 * d
    c_kv = c_q + Q_LORA
    c_kr, c_ac, c_am = 0, HEAD_PAD, HEAD_PAD + d

    x = x_ref[...].reshape(m, d)
    n = _rms(x, gmix_ref[...]).astype(BF16)

    u = _dot(n, wa_ref[:, c_u:c_u + d])
    gc = _dot(n, wa_ref[:, c_c:c_c + d])
    v = gc * u
    w0 = wconv_ref[0:1, :]
    w1 = wconv_ref[1:2, :]
    w2 = wconv_ref[2:3, :]
    ys = []
    for i in range(nb):
        vi = v[i * tm:(i + 1) * tm]

        @pl.when(j == 0)
        def _():
            vbuf_ref[i, SUBLANES - 2:SUBLANES, :] = cache_ref[i]

        vbuf_ref[i, SUBLANES:SUBLANES + tm, :] = vi
        ys.append(vbuf_ref[i, SUBLANES - 2:SUBLANES - 2 + tm, :] * w0
                  + vbuf_ref[i, SUBLANES - 1:SUBLANES - 1 + tm, :] * w1
                  + vi * w2)
        tail = vbuf_ref[i, SUBLANES + tm - 2:SUBLANES + tm, :]
        vbuf_ref[i, SUBLANES - 2:SUBLANES, :] = tail
        newconv_ref[i] = tail
    y = ys[0] if nb == 1 else jnp.concatenate(ys, axis=0)
    gb = _dot(n, wa_ref[:, c_b:c_b + d])
    ya = _dot((gb * y).astype(BF16), wco_ref[...])
    ac = _dot(n, wb_ref[:, c_ac:c_ac + d])
    ga_ref[...] = (jax.nn.sigmoid(ac) * ya).reshape(nb, tm, d)
    am = _dot(n, wb_ref[:, c_am:c_am + d])
    sgm_ref[...] = jax.nn.sigmoid(am).reshape(nb, tm, d)

    cq = _dot(n, wa_ref[:, c_q:c_q + Q_LORA])
    cqn = _rms(cq, gq_ref[...]).astype(BF16)
    qc, qa, qb = qtab_ref[0], qtab_ref[1], qtab_ref[2]
    for h in range(0, N_HEADS, HEADS_PER_DOT):
        qq = _dot(cqn, wuq_ref[:, h * HEAD_PAD:(h + HEADS_PER_DOT) * HEAD_PAD])
        for e in range(HEADS_PER_DOT):
            qh = _rope(qq[:, e * HEAD_PAD:(e + 1) * HEAD_PAD], qc, qa, qb).astype(BF16)
            for i in range(nb):
                q_ref[i, h + e] = qh[i * tm:(i + 1) * tm]

    ckv = _rms(_dot(n, wa_ref[:, c_kv:c_kv + KV_LORA]), gkv_ref[...])
    ckv_ref[...] = ckv.reshape(nb, tm, KV_LORA)
    kr = _dot(n, wb_ref[:, c_kr:c_kr + HEAD_PAD])
    kr = _rope(kr, ktab_ref[0], ktab_ref[1], ktab_ref[2])
    kr_ref[...] = kr[:, :QK_ROPE].reshape(nb, tm, QK_ROPE)
    if fuse_kv:
        def store_k(h, kh):
            k_ref[:, h] = kh.reshape(nb, tm, HEAD_PAD)

        def store_v(h, vh):
            v_ref[:, h] = vh.reshape(nb, tm, HEAD_PAD)

        _expand_kv(ckv.astype(BF16), kr, wuk_ref, wuv_ref, store_k, store_v)
    else:
        krpad_ref[...] = kr.astype(BF16).reshape(nb, tm, HEAD_PAD)


def _in_proj(x, conv_cache, g_mix, w_in_a, w_in_b, w_conv, w_co, g_q, w_uq_p, g_kv, qtab, ktab,
             w_uk_p=None, w_uv_p=None, *, nb, tm):
    fuse_kv = w_uk_p is not None
    b, t, d = x.shape
    m = nb * tm
    grid = (b // nb, t // tm)
    row = lambda bi, j: (bi, j, 0)
    head_shape = jax.ShapeDtypeStruct((b, N_HEADS, t, HEAD_PAD), BF16)
    head_spec = pl.BlockSpec((nb, N_HEADS, tm, HEAD_PAD), lambda bi, j: (bi, 0, j, 0))
    if fuse_kv:
        kv_shapes, kv_specs = (head_shape, head_shape), (head_spec, head_spec)
        kv_weights = (w_uk_p, w_uv_p)
    else:
        kv_shapes = (jax.ShapeDtypeStruct((b, t, HEAD_PAD), BF16),)
        kv_specs = (pl.BlockSpec((nb, tm, HEAD_PAD), row),)
        kv_weights = ()
    out_shape = (
        jax.ShapeDtypeStruct((b, t, d), F32),
        jax.ShapeDtypeStruct((b, t, d), F32),
        head_shape,
        jax.ShapeDtypeStruct((b, t, KV_LORA), F32),
        jax.ShapeDtypeStruct((b, t, QK_ROPE), F32),
        *kv_shapes,
        jax.ShapeDtypeStruct((b, CONV_WIDTH - 1, d), F32),
    )
    in_specs = [
        pl.BlockSpec((nb, tm, d), row),
        pl.BlockSpec((nb, CONV_WIDTH - 1, d), lambda bi, j: (bi, 0, 0)),
        _full((1, d)),
        _full(w_in_a.shape),
        _full(w_in_b.shape),
        _full((CONV_WIDTH, d)),
        _full((d, d)),
        _full((1, Q_LORA)),
        _full((Q_LORA, N_HEADS * HEAD_PAD)),
        _full((1, KV_LORA)),
        pl.BlockSpec((3, m, LANES), lambda bi, j: (0, j, 0)),
        pl.BlockSpec((3, m, LANES), lambda bi, j: (0, j, 0)),
        *[_full(w.shape) for w in kv_weights],
    ]
    out_specs = (
        pl.BlockSpec((nb, tm, d), row),
        pl.BlockSpec((nb, tm, d), row),
        head_spec,
        pl.BlockSpec((nb, tm, KV_LORA), row),
        pl.BlockSpec((nb, tm, QK_ROPE), row),
        *kv_specs,
        pl.BlockSpec((nb, CONV_WIDTH - 1, d), lambda bi, j: (bi, 0, 0)),
    )
    return pl.pallas_call(
        functools.partial(_in_proj_kernel, nb=nb, tm=tm, d=d, fuse_kv=fuse_kv),
        grid=grid, in_specs=in_specs, out_specs=out_specs, out_shape=out_shape,
        scratch_shapes=[pltpu.VMEM((nb, tm + SUBLANES, d), F32)],
        compiler_params=_params(("arbitrary", "arbitrary")),
        name="in_proj",
    )(x, conv_cache, g_mix, w_in_a, w_in_b, w_conv, w_co, g_q, w_uq_p, g_kv, qtab, ktab, *kv_weights)


def _kv_expand_kernel(ckv_ref, krpad_ref, wuk_ref, wuv_ref, k_ref, v_ref):
    def store_k(h, kh):
        k_ref[0, h] = kh

    def store_v(h, vh):
        v_ref[0, h] = vh

    _expand_kv(ckv_ref[0].astype(BF16), krpad_ref[0].astype(F32), wuk_ref, wuv_ref, store_k, store_v)


def _kv_expand(ckv, kr_pad, w_uk_p, w_uv_p, *, tl):
    b, l, _ = ckv.shape
    head_tiles = pl.BlockSpec((1, N_HEADS, tl, HEAD_PAD), lambda bi, j: (bi, 0, j, 0))
    return pl.pallas_call(
        _kv_expand_kernel,
        grid=(b, l // tl),
        in_specs=[
            pl.BlockSpec((1, tl, KV_LORA), lambda bi, j: (bi, j, 0)),
            pl.BlockSpec((1, tl, HEAD_PAD), lambda bi, j: (bi, j, 0)),
            _full((KV_LORA, N_HEADS * HEAD_PAD)),
            _full((KV_LORA, N_HEADS * HEAD_PAD)),
        ],
        out_specs=(head_tiles, head_tiles),
        out_shape=(jax.ShapeDtypeStruct((b, N_HEADS, l, HEAD_PAD), BF16),) * 2,
        compiler_params=_params(("arbitrary", "arbitrary")),
        name="kv_expand",
    )(ckv, kr_pad, w_uk_p, w_uv_p)


def _attention_kernel(q_ref, k_ref, v_ref, o_ref, m_ref, acc_ref, *, nh, tq, tk, past):
    i = pl.program_id(2)
    m_ref[...] = jnp.full(m_ref.shape, -jnp.inf, F32)
    acc_ref[...] = jnp.zeros(acc_ref.shape, F32)

    def update(start, width, mask=None):
        for e in range(nh):
            s = _dot_nt(q_ref[0, e], k_ref[0, e, pl.ds(start, width), :])
            if mask is not None:
                s = jnp.where(mask, s, -jnp.inf)
            m_old = m_ref[e]
            m_new = jnp.maximum(m_old, jnp.max(s, axis=-1, keepdims=True))
            m_wide = m_new[:, :width] if width < LANES else jnp.tile(m_new, (1, width // LANES))
            p = jnp.exp2(s - m_wide).astype(BF16)
            acc_ref[e] = (jnp.exp2(m_old - m_new) * acc_ref[e]
                          + _dot(p, v_ref[0, e, pl.ds(start, width), :]))
            m_ref[e] = m_new

    def full_blocks(n_blocks):
        def body(kb, carry):
            update(pl.multiple_of(kb * n_blocks * tk, tk), n_blocks * tk)
            return carry
        return body

    n_full = (past + i * tq) // tk
    done = 0
    for n_blocks in (4, 2, 1):
        if n_blocks * tk <= k_ref.shape[2]:
            count = (n_full - done) // n_blocks
            first_span = done // n_blocks
            lax.fori_loop(first_span, first_span + count, full_blocks(n_blocks), 0)
            done = done + count * n_blocks
    rq = lax.broadcasted_iota(jnp.int32, (tq, tq), 0) // CHUNK
    ck = lax.broadcasted_iota(jnp.int32, (tq, tq), 1) // CHUNK
    update(pl.multiple_of(past + i * tq, tq), tq, ck <= rq)
    first = lax.broadcasted_iota(jnp.int32, (tq, HEAD_PAD), 1) < V_HEAD
    for pair in range(nh // 2):
        acc0, acc1 = acc_ref[2 * pair], acc_ref[2 * pair + 1]
        num = jnp.where(first, acc0, pltpu.roll(acc1, V_HEAD, 1))
        den = jnp.where(first, pltpu.roll(acc0, V_HEAD, 1), acc1)
        o_ref[0, :, pair * HEAD_PAD:(pair + 1) * HEAD_PAD] = (num / den).astype(o_ref.dtype)


def _attention(q, k, v, *, nh, tq, tk):
    b, _, t, _ = q.shape
    l = k.shape[2]
    past = l - t
    assert nh % 2 == 0 and N_HEADS % nh == 0
    assert tq % CHUNK == 0 and all((past + i * tq) % tk == 0 for i in range(t // tq))
    kv_spec = pl.BlockSpec((1, nh, l, HEAD_PAD), lambda bi, hp, i: (bi, hp, 0, 0))
    return pl.pallas_call(
        functools.partial(_attention_kernel, nh=nh, tq=tq, tk=tk, past=past),
        grid=(b, N_HEADS // nh, t // tq),
        in_specs=[
            pl.BlockSpec((1, nh, tq, HEAD_PAD), lambda bi, hp, i: (bi, hp, i, 0)),
            kv_spec, kv_spec,
        ],
        out_specs=pl.BlockSpec((1, tq, nh * V_HEAD), lambda bi, hp, i: (bi, i, hp)),
        out_shape=jax.ShapeDtypeStruct((b, t, N_HEADS * V_HEAD), BF16),
        scratch_shapes=[pltpu.VMEM((nh, tq, LANES), F32), pltpu.VMEM((nh, tq, HEAD_PAD), F32)],
        compiler_params=_params(("arbitrary", "arbitrary", "arbitrary")),
        name="attention",
    )(q, k, v)


def _post_attn_kernel(x_ref, o_ref, ga_ref, sgm_ref, wmo_ref, wmix_ref, gmq_ref, wqm_ref,
                      mk_ref, mv_ref, wom_ref, x2_ref, om_ref, *, nb, tm, d):
    m = nb * tm
    hd = d // MEM_HEADS
    mem_scale = hd ** -0.5
    x = x_ref[...].reshape(m, d)
    yb = _dot(o_ref[...].reshape(m, d), wmo_ref[...])
    mix = ga_ref[...].reshape(m, d) + sgm_ref[...].reshape(m, d) * yb
    x1 = x + _dot(mix.astype(BF16), wmix_ref[...])
    hq = _rms(x1, gmq_ref[...]).astype(BF16)
    qm = (_dot(hq, wqm_ref[...]) * mem_scale).astype(BF16)
    for i in range(nb):
        for h in range(MEM_HEADS):
            cols = slice(h * hd, (h + 1) * hd)
            s = _dot_nt(qm[i * tm:(i + 1) * tm, cols], mk_ref[i, :, cols])
            p = jnp.exp(s - jnp.max(s, axis=-1, keepdims=True))
            l = jnp.sum(p, axis=-1, keepdims=True)
            oh = _dot(p.astype(BF16), mv_ref[i, :, cols]) / l
            om_ref[i * tm:(i + 1) * tm, cols] = oh.astype(BF16)
    x2 = x1 + _dot(om_ref[...], wom_ref[...])
    x2_ref[...] = x2.reshape(nb, tm, d)


def _post_attn(x, o, ga, sgm, w_mo, w_mix, g_mq, w_qm, mem_k, mem_v, w_om, *, nb, tm):
    b, t, d = x.shape
    mem_spec = pl.BlockSpec((nb,) + mem_k.shape[1:], lambda bi, j: (bi, 0, 0))
    row = lambda bi, j: (bi, j, 0)
    return pl.pallas_call(
        functools.partial(_post_attn_kernel, nb=nb, tm=tm, d=d),
        grid=(b // nb, t // tm),
        in_specs=[
            pl.BlockSpec((nb, tm, d), row),
            pl.BlockSpec((nb, tm, d), row),
            pl.BlockSpec((nb, tm, d), row),
            pl.BlockSpec((nb, tm, d), row),
            _full((d, d)), _full((d, d)), _full((1, d)), _full((d, d)),
            mem_spec, mem_spec,
            _full((d, d)),
        ],
        out_specs=pl.BlockSpec((nb, tm, d), row),
        out_shape=jax.ShapeDtypeStruct((b, t, d), F32),
        scratch_shapes=[pltpu.VMEM((nb * tm, d), BF16)],
        compiler_params=_params(("arbitrary", "arbitrary")),
        name="post_attn",
    )(x, o, ga, sgm, w_mo, w_mix, g_mq, w_qm, mem_k, mem_v, w_om)


def _mlp_kernel(x_ref, gmlp_ref, wup_ref, wdown_ref, gfin_ref, y_ref, *, fchunk):
    x = x_ref[...]
    hm = _rms(x, gmlp_ref[...]).astype(BF16)
    acc = x
    for c in range(wup_ref.shape[1] // fchunk):
        cols = slice(c * fchunk, (c + 1) * fchunk)
        hcol = jnp.maximum(_dot(hm, wup_ref[:, cols]), 0.0)
        acc = acc + _dot((hcol * hcol).astype(BF16), wdown_ref[cols, :])
    y_ref[...] = _rms(acc, gfin_ref[...])


def _mlp(x, g_mlp, w_up, w_down, g_final, *, tm):
    n, d = x.shape
    f = w_up.shape[1]
    return pl.pallas_call(
        functools.partial(_mlp_kernel, fchunk=min(f, 1024)),
        grid=(n // tm,),
        in_specs=[
            pl.BlockSpec((tm, d), lambda r: (r, 0)),
            _full((1, d)), _full((d, f)), _full((f, d)), _full((1, d)),
        ],
        out_specs=pl.BlockSpec((tm, d), lambda r: (r, 0)),
        out_shape=jax.ShapeDtypeStruct((n, d), F32),
        compiler_params=_params(("arbitrary",)),
        name="mlp",
    )(x, g_mlp, w_up, w_down, g_final)


def _mem_kv_kernel(mem_ref, g_ref, wk_ref, wv_ref, k_ref, v_ref, kb_ref, vb_ref):
    mn = _rms(mem_ref[0], g_ref[...]).astype(BF16)
    hd = k_ref.shape[-1]
    for w_ref, out_ref, mxu_ref in ((wk_ref, k_ref, kb_ref), (wv_ref, v_ref, vb_ref)):
        full = _dot(mn, w_ref[...])
        mxu_ref[0] = full.astype(BF16)
        for h in range(MEM_HEADS):
            out_ref[0, :, h, :] = full[:, h * hd:(h + 1) * hd]


def _mem_kv(mem, g, w_k, w_v):
    b, n_mem, d = mem.shape
    hd = d // MEM_HEADS
    blk = pl.BlockSpec((1, n_mem, d), lambda bi: (bi, 0, 0))
    out_blk = pl.BlockSpec((1, n_mem, MEM_HEADS, hd), lambda bi: (bi, 0, 0, 0))
    return pl.pallas_call(
        _mem_kv_kernel,
        grid=(b,),
        in_specs=[blk, _full((1, d)), _full((d, d)), _full((d, d))],
        out_specs=(out_blk, out_blk, blk, blk),
        out_shape=((jax.ShapeDtypeStruct((b, n_mem, MEM_HEADS, hd), F32),) * 2
                   + (jax.ShapeDtypeStruct((b, n_mem, d), BF16),) * 2),
        compiler_params=_params(("arbitrary",)),
        name="mem_kv",
    )(mem, g, w_k, w_v)


def _rope_tables(pos, scale):
    t = pos.shape[0]
    inv = ROPE_THETA ** (-jnp.arange(HALF_ROPE, dtype=F32) / HALF_ROPE)
    ang = pos.astype(F32)[:, None] * inv[None, :]
    cos, sin = jnp.cos(ang), jnp.sin(ang)
    z = lambda w: jnp.zeros((t, w), F32)
    cos_t = jnp.concatenate([cos, cos, jnp.ones((t, LANES - QK_ROPE), F32)], axis=1)
    sin_a = jnp.concatenate([z(HALF_ROPE), sin, z(LANES - QK_ROPE)], axis=1)
    sin_b = jnp.concatenate([-sin, z(LANES - HALF_ROPE)], axis=1)
    return jnp.stack([cos_t, sin_a, sin_b]) * scale


def _prep_weights(w_in, w_conv_out, w_uq, w_ukv, w_mla_out, w_mix_out, w_qm, w_om, w_up, w_down):
    d = w_in.shape[0]
    c_kr = 3 * d + Q_LORA + KV_LORA
    w_in_a = w_in[:, :c_kr]
    w_in_b = jnp.concatenate(
        [w_in[:, c_kr:c_kr + QK_ROPE], jnp.zeros((d, HEAD_PAD - QK_ROPE), w_in.dtype),
         w_in[:, c_kr + QK_ROPE:]], axis=1)
    pad = HEAD_PAD - QK_ROPE - QK_NOPE
    wq = w_uq.reshape(Q_LORA, N_HEADS, QK_NOPE + QK_ROPE)
    w_uq_p = jnp.concatenate(
        [wq[..., QK_NOPE:], wq[..., :QK_NOPE], jnp.zeros((Q_LORA, N_HEADS, pad), w_uq.dtype)],
        axis=-1).reshape(Q_LORA, N_HEADS * HEAD_PAD)
    wkv = w_ukv.reshape(KV_LORA, N_HEADS, QK_NOPE + V_HEAD)
    w_uk_p = jnp.concatenate(
        [jnp.zeros((KV_LORA, N_HEADS, QK_ROPE), w_ukv.dtype), wkv[..., :QK_NOPE],
         jnp.zeros((KV_LORA, N_HEADS, pad), w_ukv.dtype)], axis=-1).reshape(KV_LORA, N_HEADS * HEAD_PAD)
    w_uv_p = jnp.concatenate(
        [wkv[..., QK_NOPE:], jnp.zeros((KV_LORA, N_HEADS, HEAD_PAD - V_HEAD), w_ukv.dtype)],
        axis=-1).reshape(KV_LORA, N_HEADS * HEAD_PAD)
    cast = lambda w: w.astype(BF16)
    return tuple(map(cast, (w_in_a, w_in_b, w_conv_out, w_uq_p, w_uk_p, w_uv_p, w_mla_out, w_mix_out,
                            w_qm, w_om, w_up, w_down)))


def _layer(x, conv_cache, past_ckv, past_kr, mem_k, mem_v, weights, gains, w_conv, g_final,
           *, nb, tm, nh, tq, tk, tm_mlp):
    (w_in_a, w_in_b, w_co, w_uq_p, w_uk_p, w_uv_p, w_mo, w_mix, w_qm, w_om, w_up, w_down) = weights
    g_mix, g_q, g_kv, g_mem_q, g_mlp = gains
    b, t, d = x.shape
    past = past_ckv.shape[1]
    pos = past + jnp.arange(t, dtype=jnp.int32)
    qtab = jnp.tile(_rope_tables(pos, ATTN_SCALE * LOG2_E), (1, nb, 1))
    ktab = jnp.tile(_rope_tables(pos, 1.0), (1, nb, 1))
    proj_args = (x, conv_cache, g_mix, w_in_a, w_in_b, w_conv, w_co, g_q, w_uq_p, g_kv, qtab, ktab)
    if past:
        ga, sgm, q, ckv, kr, kr_pad, new_conv = _in_proj(*proj_args, nb=nb, tm=tm)
        past_kr_pad = jnp.pad(past_kr, ((0, 0), (0, 0), (0, HEAD_PAD - QK_ROPE))).astype(BF16)
        k, v = _kv_expand(jnp.concatenate([past_ckv, ckv], axis=1),
                          jnp.concatenate([past_kr_pad, kr_pad], axis=1), w_uk_p, w_uv_p, tl=past + t)
    else:
        ga, sgm, q, ckv, kr, k, v, new_conv = _in_proj(*proj_args, w_uk_p, w_uv_p, nb=nb, tm=tm)
    o = _attention(q, k, v, nh=nh, tq=tq, tk=tk)
    x2 = _post_attn(x, o, ga, sgm, w_mo, w_mix, g_mem_q, w_qm, mem_k, mem_v, w_om, nb=nb, tm=tm)
    y = _mlp(x2.reshape(b * t, d), g_mlp, w_up, w_down, g_final, tm=tm_mlp).reshape(b, t, d)
    return y, new_conv, ckv, kr


def kernel(x_prompt, x_sample, cache_conv, cache_ckv, cache_krope, cache_mem_k, cache_mem_v, mem_prompt, g_mix, w_in, w_conv, w_conv_out, g_q, w_uq, g_kv, w_ukv, w_mla_out, w_mix_out, g_mem_q, g_mem_kv, w_qm, w_km, w_vm, w_om, g_mlp, w_up, w_down, g_final):
    depth = w_in.shape[0]
    assert depth == 1, "single-layer step"
    bp, tp, d = x_prompt.shape
    bs, ts, _ = x_sample.shape
    n_mem = mem_prompt.shape[1]
    row = lambda g: g.reshape(1, -1)
    weights = _prep_weights(w_in[0], w_conv_out[0], w_uq[0], w_ukv[0], w_mla_out[0], w_mix_out[0],
                            w_qm[0], w_om[0], w_up[0], w_down[0])
    gains = (row(g_mix[0]), row(g_q[0]), row(g_kv[0]), row(g_mem_q[0]), row(g_mlp[0]))
    g_fin = row(g_final)

    mk, mv, mk_mxu, mv_mxu = _mem_kv(mem_prompt, row(g_mem_kv[0]),
                                     w_km[0].astype(BF16), w_vm[0].astype(BF16))
    y_p, conv_p, ckv_p, kr_p = _layer(
        x_prompt, jnp.zeros((bp, CONV_WIDTH - 1, d), F32), jnp.zeros((bp, 0, KV_LORA), F32),
        jnp.zeros((bp, 0, QK_ROPE), F32), mk_mxu, mv_mxu, weights, gains, w_conv[0], g_fin,
        nb=1, tm=512, nh=4, tq=1024, tk=512, tm_mlp=512)
    past = cache_ckv.shape[2]
    y_s, conv_s, ckv_s, kr_s = _layer(
        x_sample, cache_conv[0], cache_ckv[0], cache_krope[0],
        cache_mem_k[0].reshape(bs, n_mem, d).astype(BF16),
        cache_mem_v[0].reshape(bs, n_mem, d).astype(BF16),
        weights, gains, w_conv[0], g_fin,
        nb=bs, tm=ts, nh=8, tq=ts, tk=past, tm_mlp=bs * ts)
    return (y_p, y_s, conv_p[None], ckv_p[None], kr_p[None], mk[None], mv[None],
            conv_s[None], ckv_s[None], kr_s[None])
```
